```python
import jax, jax.numpy as jnp
from jax import lax
import numpy as np

D_MODEL = 2048
BATCH = 1
SEQ = 16384
DEPTH = 1

EPS = 1e-6
GMLP_WIDTH = 1024
GMLP_GROUPS = 8
GMLP_GROUP_DIM = GMLP_WIDTH // GMLP_GROUPS
GMLP_CHUNK = 128
HGRN_HEADS = 8
HGRN_KEY_DIM = 128
HGRN_VAL_DIM = 128
HGRN_QK_WIDTH = HGRN_HEADS * HGRN_KEY_DIM
HGRN_V_WIDTH = HGRN_HEADS * HGRN_VAL_DIM
HGRN_CHUNK = 16
IN_SIZES = (GMLP_WIDTH, GMLP_WIDTH, HGRN_QK_WIDTH, HGRN_QK_WIDTH, HGRN_V_WIDTH, HGRN_V_WIDTH, D_MODEL, D_MODEL)
IN_SPLITS = tuple(int(s) for s in np.cumsum(IN_SIZES)[:-1])
D_IN_PROJ = sum(IN_SIZES)
N_GROUPS = 8
EXPERTS_PER_GROUP = 8
N_EXPERTS = N_GROUPS * EXPERTS_PER_GROUP
EXPERT_TOP_K = 2
D_EXPERT = 1024
MOE_BLOCK = 128

kernel_name = "hybrid_gmlp_hgrn2_hiermoe_block"


def rms_norm(x, g):
    xf = x.astype(jnp.float32)
    y = xf * lax.rsqrt(jnp.mean(xf * xf, axis=-1, keepdims=True) + EPS)
    return (y * g.astype(jnp.float32)).astype(x.dtype)


def layer_norm(x, g, b):
    xf = x.astype(jnp.float32)
    mu = jnp.mean(xf, axis=-1, keepdims=True)
    var = jnp.mean(jnp.square(xf - mu), axis=-1, keepdims=True)
    y = (xf - mu) * lax.rsqrt(var + EPS)
    return (y * g.astype(jnp.float32) + b.astype(jnp.float32)).astype(x.dtype)


def gmlp_branch(u, v, ln_g, ln_b, w_s, b_s):
    bn, s, _ = u.shape
    nc = s // GMLP_CHUNK
    u = jax.nn.gelu(u)
    v = layer_norm(jax.nn.gelu(v), ln_g, ln_b)
    v = v.reshape(bn, nc, GMLP_CHUNK, GMLP_GROUPS, GMLP_GROUP_DIM)
    causal = jnp.tril(jnp.ones((GMLP_CHUNK, GMLP_CHUNK), dtype=bool))
    w = jnp.where(causal[None], w_s, jnp.zeros((), w_s.dtype))
    sv = jnp.einsum('gts,bnsgd->bntgd', w, v) + b_s.T[None, None, :, :, None]
    return u * sv.reshape(bn, s, GMLP_WIDTH)


def hgrn2_branch(q, f_logit, inp, g, lb, norm_g):
    bn, s, _ = q.shape
    H, K, V, C = HGRN_HEADS, HGRN_KEY_DIM, HGRN_VAL_DIM, HGRN_CHUNK
    nc = s // C
    f = lb + (1.0 - lb) * jax.nn.sigmoid(f_logit.astype(jnp.float32))
    log_f = jnp.log(f)
    k = 1.0 - f

    def chunks(t, d):
        return t.reshape(bn, nc, C, H, d).transpose(1, 0, 3, 2, 4)

    qc = chunks(q.astype(jnp.float32), K)
    kc = chunks(k, K)
    ic = chunks(inp.astype(jnp.float32), V)
    a_cum = jnp.cumsum(chunks(log_f, K), axis=3)
    a_last = a_cum[:, :, :, -1, :]

    q_dec = qc * jnp.exp(a_cum)
    k_inv = kc * jnp.exp(-a_cum)
    causal = jnp.tril(jnp.ones((C, C), dtype=bool))
    scores = jnp.einsum('nbhtk,nbhsk->nbhts', q_dec, k_inv)
    scores = jnp.where(causal, scores, 0.0)
    o_intra = jnp.einsum('nbhts,nbhsv->nbhtv', scores, ic)

    k_dec = kc * jnp.exp(a_last[:, :, :, None, :] - a_cum)

    def step(state, xs):
        q_c, k_c, i_c, al = xs
        o = jnp.einsum('bhtk,bhkv->bhtv', q_c, state)
        state = jnp.exp(al)[..., None] * state + jnp.einsum('bhsk,bhsv->bhkv', k_c, i_c)
        return state, o

    s0 = jnp.zeros((bn, H, K, V), jnp.float32)
    _, o_inter = lax.scan(step, s0, (q_dec, k_dec, ic, a_last))
    o = (o_intra + o_inter).transpose(1, 0, 3, 2, 4).reshape(bn, s, H, V)
    o = rms_norm(o, norm_g.reshape(H, V)).reshape(bn, s, H * V)
    return (o * jax.nn.silu(g.astype(jnp.float32))).astype(g.dtype)


def hier_moe(h, w_rg, b_rg, w_re, b_re, w_gate, w_up, w_down):
    bn, s, d = h.shape
    T = bn * s
    ht = h.reshape(T, d)
    g_logits = (ht @ w_rg).astype(jnp.float32) + b_rg.astype(jnp.float32)
    p_group = jax.nn.softmax(g_logits, axis=-1)
    g_sel = jnp.argmax(g_logits, axis=-1).astype(jnp.int32)
    p_g = jnp.take_along_axis(p_group, g_sel[:, None], axis=-1)
    e_logits = ((ht @ w_re).astype(jnp.float32) + b_re.astype(jnp.float32)).reshape(T, N_GROUPS, EXPERTS_PER_GROUP)
    e_in = jnp.take_along_axis(e_logits, g_sel[:, None, None], axis=1)[:, 0]
    top_v, top_i = lax.top_k(e_in, EXPERT_TOP_K)
    gate = jax.nn.softmax(top_v, axis=-1) * p_g

    n_assign = T * EXPERT_TOP_K
    eid = (g_sel[:, None] * EXPERTS_PER_GROUP + top_i.astype(jnp.int32)).reshape(-1)
    tok = jnp.repeat(jnp.arange(T, dtype=jnp.int32), EXPERT_TOP_K)
    wt = gate.reshape(-1)
    order = jnp.argsort(eid)
    e_sorted, tok_sorted, w_sorted = eid[order], tok[order], wt[order]

    counts = jax.ops.segment_sum(jnp.ones_like(eid), eid, num_segments=N_EXPERTS)
    start = jnp.cumsum(counts) - counts
    padded = ((counts + MOE_BLOCK - 1) // MOE_BLOCK) * MOE_BLOCK
    pad_start = jnp.cumsum(padded) - padded
    pad_end = pad_start + padded
    dest = pad_start[e_sorted] + (jnp.arange(n_assign, dtype=jnp.int32) - start[e_sorted])

    n_blocks = (n_assign + N_EXPERTS * (MOE_BLOCK - 1) + MOE_BLOCK - 1) // MOE_BLOCK
    P = n_blocks * MOE_BLOCK
    tok_buf = jnp.full((P,), T, dtype=jnp.int32).at[dest].set(tok_sorted)
    w_buf = jnp.zeros((P,), jnp.float32).at[dest].set(w_sorted)
    block_start = jnp.arange(n_blocks, dtype=jnp.int32) * MOE_BLOCK
    block_e = jnp.minimum(jnp.sum(pad_end[None, :] <= block_start[:, None], axis=1), N_EXPERTS - 1)

    h_pad = jnp.concatenate([ht, jnp.zeros((1, d), ht.dtype)], axis=0)
    xb = h_pad[tok_buf].reshape(n_blocks, MOE_BLOCK, d)

    def expert_block(args):
        xblk, e = args
        a = xblk @ w_gate[e]
        b = xblk @ w_up[e]
        return (jax.nn.silu(a) * b) @ w_down[e]

    yb = lax.map(expert_block, (xb, block_e)).reshape(P, d)
    y = jax.ops.segment_sum(yb * w_buf[:, None].astype(yb.dtype), tok_buf, num_segments=T + 1)[:T]
    return y.reshape(bn, s, d).astype(h.dtype)


def setup_inputs(seed: int = 0) -> dict:
    key = jax.random.key(seed)
    ks = jax.random.split(key, 24)
    f32 = jnp.float32
    L = DEPTH

    def nrm(k, shape, scale):
        return (jax.random.normal(k, shape, f32) * scale).astype(f32)

    return {
        "x": nrm(ks[0], (BATCH, SEQ, D_MODEL), 1.0),
        "norm_mix_g": 1.0 + nrm(ks[1], (L, D_MODEL), 0.02),
        "w_in": nrm(ks[2], (L, D_MODEL, D_IN_PROJ), D_MODEL ** -0.5),
        "gmlp_ln_g": 1.0 + nrm(ks[3], (L, GMLP_WIDTH), 0.02),
        "gmlp_ln_b": nrm(ks[4], (L, GMLP_WIDTH), 0.02),
        "w_spatial": nrm(ks[5], (L, GMLP_GROUPS, GMLP_CHUNK, GMLP_CHUNK), GMLP_CHUNK ** -0.5),
        "b_spatial": 1.0 + nrm(ks[6], (L, GMLP_GROUPS, GMLP_CHUNK), 0.1),
        "hgrn_lb_logits": nrm(ks[7], (L + 1, HGRN_QK_WIDTH), 0.5),
        "hgrn_norm_g": 1.0 + nrm(ks[8], (L, HGRN_V_WIDTH), 0.02),
        "w_branch_a": nrm(ks[9], (L, GMLP_WIDTH, D_MODEL), GMLP_WIDTH ** -0.5),
        "w_branch_b": nrm(ks[10], (L, HGRN_V_WIDTH, D_MODEL), HGRN_V_WIDTH ** -0.5),
        "w_out": nrm(ks[11], (L, D_MODEL, D_MODEL), D_MODEL ** -0.5),
        "norm_ffn_g": 1.0 + nrm(ks[12], (L, D_MODEL), 0.02),
        "w_router_group": nrm(ks[13], (L, D_MODEL, N_GROUPS), D_MODEL ** -0.5),
        "b_router_group": nrm(ks[14], (L, N_GROUPS), 0.01),
        "w_router_expert": nrm(ks[15], (L, D_MODEL, N_EXPERTS), D_MODEL ** -0.5),
        "b_router_expert": nrm(ks[16], (L, N_EXPERTS), 0.01),
        "w_expert_gate": nrm(ks[17], (L, N_EXPERTS, D_MODEL, D_EXPERT), D_MODEL ** -0.5),
        "w_expert_up": nrm(ks[18], (L, N_EXPERTS, D_MODEL, D_EXPERT), D_MODEL ** -0.5),
        "w_expert_down": nrm(ks[19], (L, N_EXPERTS, D_EXPERT, D_MODEL), D_EXPERT ** -0.5),
        "norm_final_g": 1.0 + nrm(ks[20], (D_MODEL,), 0.02),
    }


def reference(x, norm_mix_g, w_in, gmlp_ln_g, gmlp_ln_b, w_spatial, b_spatial, hgrn_lb_logits,
              hgrn_norm_g, w_branch_a, w_branch_b, w_out, norm_ffn_g, w_router_group, b_router_group,
              w_router_expert, b_router_expert, w_expert_gate, w_expert_up, w_expert_down, norm_final_g):
    lb_table = jnp.cumsum(jax.nn.softmax(hgrn_lb_logits.astype(jnp.float32), axis=0), axis=0)
    for l in range(DEPTH):
        h = rms_norm(x, norm_mix_g[l])
        z = h @ w_in[l]
        u, v, q, f_logit, i_in, g_out, gate_a, gate_b = jnp.split(z, IN_SPLITS, axis=-1)
        y_a = gmlp_branch(u, v, gmlp_ln_g[l], gmlp_ln_b[l], w_spatial[l], b_spatial[l])
        y_b = hgrn2_branch(q, f_logit, i_in, g_out, lb_table[l], hgrn_norm_g[l])
        merged = (jax.nn.sigmoid(gate_a) * (y_a @ w_branch_a[l])
                  + jax.nn.sigmoid(gate_b) * (y_b @ w_branch_b[l]))
        x = x + (merged @ w_out[l]).astype(x.dtype)
        h = rms_norm(x, norm_ffn_g[l])
        x = x + hier_moe(h, w_router_group[l], b_router_group[l], w_router_expert[l], b_router_expert[l],
                         w_expert_gate[l], w_expert_up[l], w_expert_down[l])
    return rms_norm(x, norm_final_g)
```

```python
import functools

import jax
import jax.numpy as jnp
import numpy as np
from jax import lax
from jax.experimental import pallas as pl
from jax.experimental.pallas import tpu as pltpu

F32 = jnp.float32
BF16 = jnp.bfloat16

D_MODEL = 2048
EPS = 1e-6
GMLP_WIDTH = 1024
GMLP_GROUPS = 8
GMLP_CHUNK = 128
HGRN_HEADS = 8
HGRN_DIM = 128
HGRN_WIDTH = HGRN_HEADS * HGRN_DIM
HGRN_CHUNK = 16
SLAB = 128
N_GROUPS = 8
EXPERTS_PER_GROUP = 8
N_EXPERTS = N_GROUPS * EXPERTS_PER_GROUP
D_EXPERT = 1024
MOE_BLOCK = 128
COL = 1024
N_COL_STEPS = 10
LANES = 128

VMEM_LIMIT = 56 * 1024 * 1024


def _cparams(sem):
    return pltpu.CompilerParams(dimension_semantics=sem, vmem_limit_bytes=VMEM_LIMIT)


def _inproj_kernel(x_ref, g_ref, w_ref, lng_ref, lnb_ref, ws_ref, bs_ref, lb_ref,
                   zc_ref, f_ref, h_scr, u_scr, v_scr):
    j = pl.program_id(1)
    tm = x_ref.shape[0]

    @pl.when(j == 0)
    def _():
        x = x_ref[...]
        ms = jnp.mean(x * x, axis=-1, keepdims=True)
        h_scr[...] = (x * lax.rsqrt(ms + EPS) * g_ref[...]).astype(BF16)

    acc = jnp.dot(h_scr[...], w_ref[...], preferred_element_type=F32)

    @pl.when(j == 0)
    def _():
        u_scr[...] = jax.nn.gelu(acc)

    @pl.when(j == 1)
    def _():
        gv = jax.nn.gelu(acc)
        mu = jnp.mean(gv, axis=-1, keepdims=True)
        var = jnp.mean(jnp.square(gv - mu), axis=-1, keepdims=True)
        vn = (gv - mu) * lax.rsqrt(var + EPS) * lng_ref[...] + lnb_ref[...]
        v_scr[...] = vn.astype(BF16)
        for c in range(tm // GMLP_CHUNK):
            rows = slice(c * GMLP_CHUNK, (c + 1) * GMLP_CHUNK)
            for g in range(GMLP_GROUPS):
                cols = slice(g * LANES, (g + 1) * LANES)
                sv = jnp.dot(ws_ref[g], v_scr[rows, cols], preferred_element_type=F32)
                sv = sv + bs_ref[:, cols]
                zc_ref[rows, cols] = (u_scr[rows, cols] * sv).astype(BF16)

    @pl.when(j == 2)
    def _():
        zc_ref[...] = acc.astype(BF16)

    @pl.when(j == 3)
    def _():
        lb = lb_ref[...]
        f_ref[...] = lb + (1.0 - lb) * jax.nn.sigmoid(acc)

    @pl.when(j == 4)
    def _():
        zc_ref[...] = acc.astype(BF16)

    @pl.when(j == 5)
    def _():
        zc_ref[...] = jax.nn.silu(acc).astype(BF16)

    @pl.when(j >= 6)
    def _():
        zc_ref[...] = jax.nn.sigmoid(acc).astype(BF16)


def _zc_col(j):
    return jnp.where(j < 2, 0, jnp.where(j < 4, 1, j - 2))


def _inproj(x2, g, w_bf, lng, lnb, ws_bf, bs_full, lb, tm):
    T = x2.shape[0]
    grid = (T // tm, N_COL_STEPS)
    return pl.pallas_call(
        _inproj_kernel,
        grid=grid,
        in_specs=[
            pl.BlockSpec((tm, D_MODEL), lambda i, j: (i, 0)),
            pl.BlockSpec((1, D_MODEL), lambda i, j: (0, 0)),
            pl.BlockSpec((D_MODEL, COL), lambda i, j: (0, j)),
            pl.BlockSpec((1, COL), lambda i, j: (0, 0)),
            pl.BlockSpec((1, COL), lambda i, j: (0, 0)),
            pl.BlockSpec((GMLP_GROUPS, GMLP_CHUNK, GMLP_CHUNK), lambda i, j: (0, 0, 0)),
            pl.BlockSpec((GMLP_CHUNK, COL), lambda i, j: (0, 0)),
            pl.BlockSpec((1, COL), lambda i, j: (0, 0)),
        ],
        out_specs=[
            pl.BlockSpec((tm, COL), lambda i, j: (i, _zc_col(j))),
            pl.BlockSpec((tm, COL), lambda i, j: (i, 0)),
        ],
        out_shape=[
            jax.ShapeDtypeStruct((T, 8 * COL), BF16),
            jax.ShapeDtypeStruct((T, COL), F32),
        ],
        scratch_shapes=[
            pltpu.VMEM((tm, D_MODEL), BF16),
            pltpu.VMEM((tm, COL), F32),
            pltpu.VMEM((tm, COL), BF16),
        ],
        compiler_params=_cparams(("arbitrary", "arbitrary")),
        name="inproj",
    )(x2, g, w_bf, lng, lnb, ws_bf, bs_full, lb)


_N_CHUNKS = SLAB // HGRN_CHUNK


def _hgrn_kernel(q_ref, f_ref, i_ref, sg_ref, ng_ref, l16_ref, o_ref, st_scr):
    step = pl.program_id(0)

    @pl.when(step == 0)
    def _():
        st_scr[...] = jnp.zeros_like(st_scr)

    rows_per_step = q_ref.shape[0]
    row = lax.broadcasted_iota(jnp.int32, (SLAB, SLAB), 0)
    col = lax.broadcasted_iota(jnp.int32, (SLAB, SLAB), 1)
    rc = row // HGRN_CHUNK
    cc = col // HGRN_CHUNK
    masks = [(rc == cc) & (col <= row)]
    for nb in (1, 2, 4):
        rs = rc // nb
        cs = cc // nb
        masks.append(((rs % 2) == 1) & (cs == rs - 1))

    for sl in range(rows_per_step // SLAB):
        r0 = sl * SLAB
        f = f_ref[r0:r0 + SLAB, :]
        lf = jnp.log(f)
        kk = 1.0 - f
        lf_hi = lf.astype(BF16)
        lf_lo = (lf - lf_hi.astype(F32)).astype(BF16)
        l16 = l16_ref[...]
        e16 = (jnp.dot(l16, lf_hi, preferred_element_type=F32)
               + jnp.dot(l16, lf_lo, preferred_element_type=F32))
        tot = [e16[c * HGRN_CHUNK + HGRN_CHUNK - 1:c * HGRN_CHUNK + HGRN_CHUNK, :]
               for c in range(_N_CHUNKS)]
        r16 = jnp.concatenate(
            [tot[c] - e16[c * HGRN_CHUNK:(c + 1) * HGRN_CHUNK, :] for c in range(_N_CHUNKS)], axis=0)
        qf = q_ref[r0:r0 + SLAB, :].astype(F32)
        q_dec = qf * jnp.exp(e16)
        k_inv = kk * jnp.exp(-e16)
        k_dec = kk * jnp.exp(r16)

        def chunk_sum(lo, hi):
            acc = None
            for c in range(lo, hi):
                acc = tot[c] if acc is None else acc + tot[c]
            return acc

        def scaled(base, offsets):
            parts = []
            for c in range(_N_CHUNKS):
                blk = base[c * HGRN_CHUNK:(c + 1) * HGRN_CHUNK, :]
                if offsets[c] is not None:
                    blk = blk * jnp.exp(offsets[c])
                parts.append(blk)
            return jnp.concatenate(parts, axis=0)

        q_lv = [q_dec, q_dec]
        k_lv = [k_inv, k_dec]
        for nb in (2, 4, 8):
            q_off = [chunk_sum((c // nb) * nb, c) for c in range(_N_CHUNKS)]
            k_off = [chunk_sum(c + 1, (c // nb + 1) * nb) for c in range(_N_CHUNKS)]
            q_lv.append(scaled(q_dec, q_off))
            k_lv.append(scaled(k_dec, k_off))
        slab_decay = jnp.exp(chunk_sum(0, _N_CHUNKS))

        q_bf = [a.astype(BF16) for a in q_lv]
        k_bf = [a.astype(BF16) for a in k_lv]
        i_bf = i_ref[r0:r0 + SLAB, :]

        for h in range(HGRN_HEADS):
            cols = slice(h * HGRN_DIM, (h + 1) * HGRN_DIM)
            scores = None
            for lv in range(4):
                s_lv = lax.dot_general(q_bf[lv][:, cols], k_bf[lv][:, cols],
                                       (((1,), (1,)), ((), ())), preferred_element_type=F32)
                s_lv = jnp.where(masks[lv], s_lv, 0.0)
                scores = s_lv if scores is None else scores + s_lv
            ih = i_bf[:, cols]
            st = st_scr[h]
            o = jnp.dot(scores.astype(BF16), ih, preferred_element_type=F32)
            o = o + lax.dot_general(q_bf[4][:, cols], st.astype(BF16),
                                    (((1,), (1,)), ((), ())), preferred_element_type=F32)
            upd = lax.dot_general(ih, k_bf[4][:, cols],
                                  (((0,), (0,)), ((), ())), preferred_element_type=F32)
            st_scr[h] = st * slab_decay[:, cols] + upd
            ms = jnp.mean(o * o, axis=-1, keepdims=True)
            on = o * lax.rsqrt(ms + EPS) * ng_ref[:, cols]
            o_ref[r0:r0 + SLAB, cols] = (on * sg_ref[r0:r0 + SLAB, cols].astype(F32)).astype(BF16)


def _hgrn(zc, f, ng, l16, rows):
    T = f.shape[0]
    return pl.pallas_call(
        _hgrn_kernel,
        grid=(T // rows,),
        in_specs=[
            pl.BlockSpec((rows, COL), lambda i: (i, 1)),
            pl.BlockSpec((rows, COL), lambda i: (i, 0)),
            pl.BlockSpec((rows, COL), lambda i: (i, 2)),
            pl.BlockSpec((rows, COL), lambda i: (i, 3)),
            pl.BlockSpec((1, COL), lambda i: (0, 0)),
            pl.BlockSpec((SLAB, SLAB), lambda i: (0, 0)),
        ],
        out_specs=pl.BlockSpec((rows, COL), lambda i: (i, 0)),
        out_shape=jax.ShapeDtypeStruct((T, COL), BF16),
        scratch_shapes=[pltpu.VMEM((HGRN_HEADS, HGRN_DIM, HGRN_DIM), F32)],
        compiler_params=_cparams(("arbitrary",)),
        name="hgrn",
    )(zc, f, zc, zc, ng, l16)


_R_E1, _R_E2, _R_W1, _R_W2, _R_RANK1, _R_RANK2 = range(6)
_GROUP_LANE0 = N_EXPERTS


def _merge_kernel(ya_ref, yb_ref, ga0_ref, ga1_ref, gb0_ref, gb1_ref, x_ref,
                  wa_ref, wb_ref, wo_ref, g_ref, wr_ref, br_ref, tri_ref,
                  x1_ref, h2_ref, route_ref, cnt_ref, cnt_scr):
    step = pl.program_id(0)

    @pl.when(step == 0)
    def _():
        cnt_scr[...] = jnp.zeros_like(cnt_scr)

    pa = jnp.dot(ya_ref[...], wa_ref[...], preferred_element_type=F32)
    pb = jnp.dot(yb_ref[...], wb_ref[...], preferred_element_type=F32)
    ga = jnp.concatenate([ga0_ref[...], ga1_ref[...]], axis=1).astype(F32)
    gb = jnp.concatenate([gb0_ref[...], gb1_ref[...]], axis=1).astype(F32)
    merged = ga * pa + gb * pb
    x1 = x_ref[...] + jnp.dot(merged.astype(BF16), wo_ref[...], preferred_element_type=F32)
    x1_ref[...] = x1
    ms = jnp.mean(x1 * x1, axis=-1, keepdims=True)
    h2 = x1 * lax.rsqrt(ms + EPS) * g_ref[...]
    h2_ref[...] = h2
    logits = jnp.dot(h2.astype(BF16), wr_ref[...], preferred_element_type=F32) + br_ref[...]

    tm = logits.shape[0]
    lane = lax.broadcasted_iota(jnp.int32, (tm, LANES), 1).astype(F32)
    neg = jnp.float32(-jnp.inf)
    big = jnp.float32(1e9)
    is_group = (lane >= _GROUP_LANE0) & (lane < _GROUP_LANE0 + N_GROUPS)
    gl = jnp.where(is_group, logits, neg)
    gmax = jnp.max(gl, axis=-1, keepdims=True)
    g_sel = jnp.min(jnp.where(gl == gmax, lane, big), axis=-1, keepdims=True) - _GROUP_LANE0
    p_g = 1.0 / jnp.sum(jnp.exp(gl - gmax), axis=-1, keepdims=True)
    lo = g_sel * EXPERTS_PER_GROUP
    in_group = (lane >= lo) & (lane < lo + EXPERTS_PER_GROUP)
    el = jnp.where(in_group, logits, neg)
    m1 = jnp.max(el, axis=-1, keepdims=True)
    e1 = jnp.min(jnp.where(el == m1, lane, big), axis=-1, keepdims=True)
    el2 = jnp.where(lane == e1, neg, el)
    m2 = jnp.max(el2, axis=-1, keepdims=True)
    e2 = jnp.min(jnp.where(el2 == m2, lane, big), axis=-1, keepdims=True)
    ex = jnp.exp(m2 - m1)
    w1 = p_g / (1.0 + ex)
    w2 = p_g * ex / (1.0 + ex)

    oh1 = (lane == e1).astype(F32)
    oh2 = (lane == e2).astype(F32)
    both = oh1 + oh2
    before = jnp.dot(tri_ref[...], both.astype(BF16), preferred_element_type=F32) + cnt_scr[...]
    rank1 = jnp.sum(oh1 * before, axis=-1, keepdims=True)
    rank2 = jnp.sum(oh2 * before, axis=-1, keepdims=True)
    cnt = cnt_scr[...] + jnp.sum(both, axis=0, keepdims=True)
    cnt_scr[...] = cnt
    cnt_ref[...] = cnt

    rec = jnp.zeros((tm, LANES), F32)
    for idx, val in ((_R_E1, e1), (_R_E2, e2), (_R_W1, w1), (_R_W2, w2),
                     (_R_RANK1, rank1), (_R_RANK2, rank2)):
        rec = jnp.where(lane == idx, val, rec)
    route_ref[...] = rec


def _merge(zc, yb, x2, wa, wb, wo, g, wr, br, tri, tm):
    T = x2.shape[0]
    const = lambda shape: pl.BlockSpec(shape, lambda i: (0,) * len(shape),
                                       pipeline_mode=pl.Buffered(1))
    return pl.pallas_call(
        _merge_kernel,
        grid=(T // tm,),
        in_specs=[
            pl.BlockSpec((tm, COL), lambda i: (i, 0)),
            pl.BlockSpec((tm, COL), lambda i: (i, 0)),
            pl.BlockSpec((tm, COL), lambda i: (i, 4)),
            pl.BlockSpec((tm, COL), lambda i: (i, 5)),
            pl.BlockSpec((tm, COL), lambda i: (i, 6)),
            pl.BlockSpec((tm, COL), lambda i: (i, 7)),
            pl.BlockSpec((tm, D_MODEL), lambda i: (i, 0)),
            const((GMLP_WIDTH, D_MODEL)),
            const((HGRN_WIDTH, D_MODEL)),
            const((D_MODEL, D_MODEL)),
            const((1, D_MODEL)),
            const((D_MODEL, LANES)),
            const((1, LANES)),
            const((tm, tm)),
        ],
        out_specs=[
            pl.BlockSpec((tm, D_MODEL), lambda i: (i, 0)),
            pl.BlockSpec((tm, D_MODEL), lambda i: (i, 0)),
            pl.BlockSpec((tm, LANES), lambda i: (i, 0)),
            pl.BlockSpec((1, LANES), lambda i: (0, 0)),
        ],
        out_shape=[
            jax.ShapeDtypeStruct((T, D_MODEL), F32),
            jax.ShapeDtypeStruct((T, D_MODEL), F32),
            jax.ShapeDtypeStruct((T, LANES), F32),
            jax.ShapeDtypeStruct((1, LANES), F32),
        ],
        scratch_shapes=[pltpu.VMEM((1, LANES), F32)],
        compiler_params=_cparams(("arbitrary",)),
        name="merge_route",
    )(zc, yb, zc, zc, zc, zc, x2, wa, wb, wo, g, wr, br, tri)


def _dispatch_kernel(dest_ref, h_ref, xb_in_ref, xb_ref, sem):
    del xb_in_ref
    tb = dest_ref.shape[2] // 2
    base = pl.program_id(0) * tb

    def row_copy(t, k):
        d = dest_ref[0, 0, 2 * t + k]
        return pltpu.make_async_copy(h_ref.at[pl.ds(base + t, 1)], xb_ref.at[pl.ds(d, 1)], sem)

    def start(t, c):
        row_copy(t, 0).start()
        row_copy(t, 1).start()
        return c

    def wait(t, c):
        row_copy(t, 0).wait()
        row_copy(t, 1).wait()
        return c

    lax.fori_loop(0, tb, start, 0)
    lax.fori_loop(0, tb, wait, 0)


def _dispatch(dest3, h2, xb0, tb):
    T = h2.shape[0]
    return pl.pallas_call(
        _dispatch_kernel,
        grid=(T // tb,),
        in_specs=[
            pl.BlockSpec((1, 1, 2 * tb), lambda i: (i, 0, 0), memory_space=pltpu.SMEM),
            pl.BlockSpec(memory_space=pl.ANY),
            pl.BlockSpec(memory_space=pl.ANY),
        ],
        out_specs=pl.BlockSpec(memory_space=pl.ANY),
        out_shape=jax.ShapeDtypeStruct(xb0.shape, xb0.dtype),
        scratch_shapes=[pltpu.SemaphoreType.DMA(())],
        input_output_aliases={2: 0},
        compiler_params=_cparams(("arbitrary",)),
        name="dispatch",
    )(dest3, h2, xb0)


def _moe_kernel(be_ref, nused_ref, x_ref, wg_ref, wu_ref, wd_ref, y_ref):
    del be_ref
    used = pl.program_id(0) < nused_ref[0]

    @pl.when(used)
    def _():
        x = x_ref[...].astype(BF16)
        a = jnp.dot(x, wg_ref[0], preferred_element_type=F32)
        b = jnp.dot(x, wu_ref[0], preferred_element_type=F32)
        hdn = (jax.nn.silu(a) * b).astype(BF16)
        y_ref[...] = jnp.dot(hdn, wd_ref[0], preferred_element_type=F32)

    @pl.when(jnp.logical_not(used))
    def _():
        y_ref[...] = jnp.zeros_like(y_ref)


def _moe(block_e, n_used, xb, wg, wu, wd):
    P = xb.shape[0]
    n_blocks = P // MOE_BLOCK

    def blk(b, be, nu):
        return jnp.minimum(b, nu[0] - 1)

    grid_spec = pltpu.PrefetchScalarGridSpec(
        num_scalar_prefetch=2,
        grid=(n_blocks,),
        in_specs=[
            pl.BlockSpec((MOE_BLOCK, D_MODEL), lambda b, be, nu: (blk(b, be, nu), 0)),
            pl.BlockSpec((1, D_MODEL, D_EXPERT), lambda b, be, nu: (be[blk(b, be, nu)], 0, 0)),
            pl.BlockSpec((1, D_MODEL, D_EXPERT), lambda b, be, nu: (be[blk(b, be, nu)], 0, 0)),
            pl.BlockSpec((1, D_EXPERT, D_MODEL), lambda b, be, nu: (be[blk(b, be, nu)], 0, 0)),
        ],
        out_specs=pl.BlockSpec((MOE_BLOCK, D_MODEL), lambda b, be, nu: (b, 0)),
    )
    return pl.pallas_call(
        _moe_kernel,
        grid_spec=grid_spec,
        out_shape=jax.ShapeDtypeStruct((P, D_MODEL), F32),
        compiler_params=_cparams(("arbitrary",)),
        name="moe_experts",
    )(block_e, n_used, xb, wg, wu, wd)


def _combine_kernel(dest_ref, yb_ref, x1_ref, route_ref, g_ref, out_ref, buf, sem):
    tb = x1_ref.shape[0]

    def row_copy(t, k):
        d = dest_ref[0, 0, 2 * t + k]
        return pltpu.make_async_copy(yb_ref.at[pl.ds(d, 1)], buf.at[k, pl.ds(t, 1)], sem)

    def start(t, c):
        row_copy(t, 0).start()
        row_copy(t, 1).start()
        return c

    def wait(t, c):
        row_copy(t, 0).wait()
        row_copy(t, 1).wait()
        return c

    lax.fori_loop(0, tb, start, 0)
    lax.fori_loop(0, tb, wait, 0)

    rec = route_ref[...]
    w1 = rec[:, _R_W1:_R_W1 + 1]
    w2 = rec[:, _R_W2:_R_W2 + 1]
    x2 = x1_ref[...] + (buf[0] * w1 + buf[1] * w2)
    ms = jnp.mean(x2 * x2, axis=-1, keepdims=True)
    out_ref[...] = x2 * lax.rsqrt(ms + EPS) * g_ref[...]


def _combine(dest3, yb, x1, route, g, tb):
    T = x1.shape[0]
    return pl.pallas_call(
        _combine_kernel,
        grid=(T // tb,),
        in_specs=[
            pl.BlockSpec((1, 1, 2 * tb), lambda i: (i, 0, 0), memory_space=pltpu.SMEM),
            pl.BlockSpec(memory_space=pl.ANY),
            pl.BlockSpec((tb, D_MODEL), lambda i: (i, 0)),
            pl.BlockSpec((tb, LANES), lambda i: (i, 0)),
            pl.BlockSpec((1, D_MODEL), lambda i: (0, 0)),
        ],
        out_specs=pl.BlockSpec((tb, D_MODEL), lambda i: (i, 0)),
        out_shape=jax.ShapeDtypeStruct((T, D_MODEL), F32),
        scratch_shapes=[pltpu.VMEM((2, tb, D_MODEL), F32), pltpu.SemaphoreType.DMA(())],
        compiler_params=_cparams(("arbitrary",)),
        name="combine",
    )(dest3, yb, x1, route, g)


def _tile(T, pref):
    return pref if T % pref == 0 else T


def kernel(x, norm_mix_g, w_in, gmlp_ln_g, gmlp_ln_b, w_spatial, b_spatial, hgrn_lb_logits,
           hgrn_norm_g, w_branch_a, w_branch_b, w_out, norm_ffn_g, w_router_group, b_router_group,
           w_router_expert, b_router_expert, w_expert_gate, w_expert_up, w_expert_down, norm_final_g):
    bn, s, d = x.shape
    T = bn * s
    x2 = x.reshape(T, d)
    l = 0

    lb = jnp.cumsum(jax.nn.softmax(hgrn_lb_logits.astype(F32), axis=0), axis=0)[l][None, :]
    causal = jnp.tril(jnp.ones((GMLP_CHUNK, GMLP_CHUNK), dtype=bool))
    ws_bf = jnp.where(causal[None], w_spatial[l], 0.0).astype(BF16)
    bs_full = jnp.repeat(b_spatial[l].T, GMLP_WIDTH // GMLP_GROUPS, axis=1)
    r = np.arange(SLAB)
    l16 = jnp.asarray(((r[:, None] // HGRN_CHUNK == r[None, :] // HGRN_CHUNK)
                       & (r[None, :] <= r[:, None])).astype(np.float32), dtype=BF16)
    wr = jnp.concatenate([w_router_expert[l], w_router_group[l],
                          jnp.zeros((d, LANES - N_EXPERTS - N_GROUPS), F32)], axis=1).astype(BF16)
    br = jnp.concatenate([b_router_expert[l], b_router_group[l],
                          jnp.zeros((LANES - N_EXPERTS - N_GROUPS,), F32)])[None, :].astype(F32)

    tm1 = _tile(T, 512)
    zc, f = _inproj(x2, norm_mix_g[l][None, :], w_in[l].astype(BF16), gmlp_ln_g[l][None, :],
                    gmlp_ln_b[l][None, :], ws_bf, bs_full, lb, tm1)

    y_b = _hgrn(zc, f, hgrn_norm_g[l][None, :], l16, _tile(T, 256))

    tm3 = _tile(T, 256)
    rr = np.arange(tm3)
    tri = jnp.asarray((rr[None, :] < rr[:, None]).astype(np.float32), dtype=BF16)
    x1, h2, route, counts = _merge(
        zc, y_b, x2, w_branch_a[l].astype(BF16), w_branch_b[l].astype(BF16), w_out[l].astype(BF16),
        norm_ffn_g[l][None, :], wr, br, tri, tm3)

    counts = counts[0, :N_EXPERTS].astype(jnp.int32)
    padded = ((counts + MOE_BLOCK - 1) // MOE_BLOCK) * MOE_BLOCK
    pad_end = jnp.cumsum(padded)
    pad_start = pad_end - padded
    n_assign = 2 * T
    n_blocks = (n_assign + N_EXPERTS * (MOE_BLOCK - 1) + MOE_BLOCK - 1) // MOE_BLOCK
    block_start = jnp.arange(n_blocks, dtype=jnp.int32) * MOE_BLOCK
    block_e = jnp.minimum(jnp.sum(pad_end[None, :] <= block_start[:, None], axis=1),
                          N_EXPERTS - 1).astype(jnp.int32)
    n_used = (pad_end[-1:] // MOE_BLOCK).astype(jnp.int32)
    eid = route[:, _R_E1:_R_E2 + 1].astype(jnp.int32)
    rank = route[:, _R_RANK1:_R_RANK2 + 1].astype(jnp.int32)
    dest = pad_start[eid] + rank

    tb = _tile(T, 256)
    dest3 = dest.reshape(T // tb, 1, 2 * tb)
    xb = _dispatch(dest3, h2, jnp.zeros((n_blocks * MOE_BLOCK, d), F32), tb)
    yb = _moe(block_e, n_used, xb, w_expert_gate[l].astype(BF16), w_expert_up[l].astype(BF16),
              w_expert_down[l].astype(BF16))
    out = _combine(dest3, yb, x1, route, norm_final_g[None, :], tb)
    return out.reshape(bn, s, d)
```

```python
import functools

import jax
import jax.numpy as jnp
import numpy as np
from jax import lax
from jax.experimental import pallas as pl
from jax.experimental.pallas import tpu as pltpu

F32 = jnp.float32
BF16 = jnp.bfloat16

D_MODEL = 2048
EPS = 1e-6
GMLP_WIDTH = 1024
GMLP_GROUPS = 8
GMLP_CHUNK = 128
HGRN_HEADS = 8
HGRN_DIM = 128
HGRN_WIDTH = HGRN_HEADS * HGRN_DIM
HGRN_CHUNK = 16
SLAB = 128
N_GROUPS = 8
EXPERTS_PER_GROUP = 8
N_EXPERTS = N_GROUPS * EXPERTS_PER_GROUP
D_EXPERT = 1024
MOE_BLOCK = 128
COL = 1024
N_COL_STEPS = 10
LANES = 128

VMEM_LIMIT = 56 * 1024 * 1024


def _cparams(sem):
    return pltpu.CompilerParams(dimension_semantics=sem, vmem_limit_bytes=VMEM_LIMIT)


def _inproj_kernel(x_ref, g_ref, w_ref, lng_ref, lnb_ref, ws_ref, bs_ref, lb_ref,
                   zc_ref, f_ref, h_scr, u_scr, v_scr):
    j = pl.program_id(1)
    tm = x_ref.shape[0]

    @pl.when(j == 0)
    def _():
        x = x_ref[...]
        ms = jnp.mean(x * x, axis=-1, keepdims=True)
        h_scr[...] = (x * lax.rsqrt(ms + EPS) * g_ref[...]).astype(BF16)

    acc = jnp.dot(h_scr[...], w_ref[...], preferred_element_type=F32)

    @pl.when(j == 0)
    def _():
        u_scr[...] = jax.nn.gelu(acc)

    @pl.when(j == 1)
    def _():
        gv = jax.nn.gelu(acc)
        mu = jnp.mean(gv, axis=-1, keepdims=True)
        var = jnp.mean(jnp.square(gv - mu), axis=-1, keepdims=True)
        vn = (gv - mu) * lax.rsqrt(var + EPS) * lng_ref[...] + lnb_ref[...]
        v_scr[...] = vn.astype(BF16)
        for c in range(tm // GMLP_CHUNK):
            rows = slice(c * GMLP_CHUNK, (c + 1) * GMLP_CHUNK)
            for g in range(GMLP_GROUPS):
                cols = slice(g * LANES, (g + 1) * LANES)
                sv = jnp.dot(ws_ref[g], v_scr[rows, cols], preferred_element_type=F32)
                sv = sv + bs_ref[:, cols]
                zc_ref[rows, cols] = (u_scr[rows, cols] * sv).astype(BF16)

    @pl.when(j == 2)
    def _():
        zc_ref[...] = acc.astype(BF16)

    @pl.when(j == 3)
    def _():
        lb = lb_ref[...]
        f_ref[...] = lb + (1.0 - lb) * jax.nn.sigmoid(acc)

    @pl.when(j == 4)
    def _():
        zc_ref[...] = acc.astype(BF16)

    @pl.when(j == 5)
    def _():
        zc_ref[...] = jax.nn.silu(acc).astype(BF16)

    @pl.when(j >= 6)
    def _():
        zc_ref[...] = jax.nn.sigmoid(acc).astype(BF16)


def _zc_col(j):
    return jnp.where(j < 2, 0, jnp.where(j < 4, 1, j - 2))


def _inproj(x2, g, w_bf, lng, lnb, ws_bf, bs_full, lb, tm):
    T = x2.shape[0]
    grid = (T // tm, N_COL_STEPS)
    return pl.pallas_call(
        _inproj_kernel,
        grid=grid,
        in_specs=[
            pl.BlockSpec((tm, D_MODEL), lambda i, j: (i, 0)),
            pl.BlockSpec((1, D_MODEL), lambda i, j: (0, 0)),
            pl.BlockSpec((D_MODEL, COL), lambda i, j: (0, j)),
            pl.BlockSpec((1, COL), lambda i, j: (0, 0)),
            pl.BlockSpec((1, COL), lambda i, j: (0, 0)),
            pl.BlockSpec((GMLP_GROUPS, GMLP_CHUNK, GMLP_CHUNK), lambda i, j: (0, 0, 0)),
            pl.BlockSpec((GMLP_CHUNK, COL), lambda i, j: (0, 0)),
            pl.BlockSpec((1, COL), lambda i, j: (0, 0)),
        ],
        out_specs=[
            pl.BlockSpec((tm, COL), lambda i, j: (i, _zc_col(j))),
            pl.BlockSpec((tm, COL), lambda i, j: (i, 0)),
        ],
        out_shape=[
            jax.ShapeDtypeStruct((T, 8 * COL), BF16),
            jax.ShapeDtypeStruct((T, COL), F32),
        ],
        scratch_shapes=[
            pltpu.VMEM((tm, D_MODEL), BF16),
            pltpu.VMEM((tm, COL), F32),
            pltpu.VMEM((tm, COL), BF16),
        ],
        compiler_params=_cparams(("arbitrary", "arbitrary")),
        name="inproj",
    )(x2, g, w_bf, lng, lnb, ws_bf, bs_full, lb)


_N_CHUNKS = SLAB // HGRN_CHUNK


def _hgrn_kernel(q_ref, f_ref, i_ref, sg_ref, ng_ref, l16_ref, o_ref, st_scr):
    step = pl.program_id(0)

    @pl.when(step == 0)
    def _():
        st_scr[...] = jnp.zeros_like(st_scr)

    rows_per_step = q_ref.shape[0]
    row = lax.broadcasted_iota(jnp.int32, (SLAB, SLAB), 0)
    col = lax.broadcasted_iota(jnp.int32, (SLAB, SLAB), 1)
    rc = row // HGRN_CHUNK
    cc = col // HGRN_CHUNK
    masks = [(rc == cc) & (col <= row)]
    for nb in (1, 2, 4):
        rs = rc // nb
        cs = cc // nb
        masks.append(((rs % 2) == 1) & (cs == rs - 1))

    for sl in range(rows_per_step // SLAB):
        r0 = sl * SLAB
        f = f_ref[r0:r0 + SLAB, :]
        lf = jnp.log(f)
        kk = 1.0 - f
        lf_hi = lf.astype(BF16)
        lf_lo = (lf - lf_hi.astype(F32)).astype(BF16)
        l16 = l16_ref[...]
        e16 = (jnp.dot(l16, lf_hi, preferred_element_type=F32)
               + jnp.dot(l16, lf_lo, preferred_element_type=F32))
        tot = [e16[c * HGRN_CHUNK + HGRN_CHUNK - 1:c * HGRN_CHUNK + HGRN_CHUNK, :]
               for c in range(_N_CHUNKS)]
        r16 = jnp.concatenate(
            [tot[c] - e16[c * HGRN_CHUNK:(c + 1) * HGRN_CHUNK, :] for c in range(_N_CHUNKS)], axis=0)
        qf = q_ref[r0:r0 + SLAB, :].astype(F32)
        q_dec = qf * jnp.exp(e16)
        k_inv = kk * jnp.exp(-e16)
        k_dec = kk * jnp.exp(r16)

        def chunk_sum(lo, hi):
            acc = None
            for c in range(lo, hi):
                acc = tot[c] if acc is None else acc + tot[c]
            return acc

        def scaled(base, offsets):
            parts = []
            for c in range(_N_CHUNKS):
                blk = base[c * HGRN_CHUNK:(c + 1) * HGRN_CHUNK, :]
                if offsets[c] is not None:
                    blk = blk * jnp.exp(offsets[c])
                parts.append(blk)
            return jnp.concatenate(parts, axis=0)

        q_lv = [q_dec, q_dec]
        k_lv = [k_inv, k_dec]
        for nb in (2, 4, 8):
            q_off = [chunk_sum((c // nb) * nb, c) for c in range(_N_CHUNKS)]
            k_off = [chunk_sum(c + 1, (c // nb + 1) * nb) for c in range(_N_CHUNKS)]
            q_lv.append(scaled(q_dec, q_off))
            k_lv.append(scaled(k_dec, k_off))
        slab_decay = jnp.exp(chunk_sum(0, _N_CHUNKS))

        q_bf = [a.astype(BF16) for a in q_lv]
        k_bf = [a.astype(BF16) for a in k_lv]
        i_bf = i_ref[r0:r0 + SLAB, :]

        for h in range(HGRN_HEADS):
            cols = slice(h * HGRN_DIM, (h + 1) * HGRN_DIM)
            scores = None
            for lv in range(4):
                s_lv = lax.dot_general(q_bf[lv][:, cols], k_bf[lv][:, cols],
                                       (((1,), (1,)), ((), ())), preferred_element_type=F32)
                s_lv = jnp.where(masks[lv], s_lv, 0.0)
                scores = s_lv if scores is None else scores + s_lv
            ih = i_bf[:, cols]
            st = st_scr[h]
            o = jnp.dot(scores.astype(BF16), ih, preferred_element_type=F32)
            o = o + lax.dot_general(q_bf[4][:, cols], st.astype(BF16),
                                    (((1,), (1,)), ((), ())), preferred_element_type=F32)
            upd = lax.dot_general(ih, k_bf[4][:, cols],
                                  (((0,), (0,)), ((), ())), preferred_element_type=F32)
            st_scr[h] = st * slab_decay[:, cols] + upd
            ms = jnp.mean(o * o, axis=-1, keepdims=True)
            on = o * lax.rsqrt(ms + EPS) * ng_ref[:, cols]
            o_ref[r0:r0 + SLAB, cols] = (on * sg_ref[r0:r0 + SLAB, cols].astype(F32)).astype(BF16)


def _hgrn(zc, f, ng, l16, rows):
    T = f.shape[0]
    return pl.pallas_call(
        _hgrn_kernel,
        grid=(T // rows,),
        in_specs=[
            pl.BlockSpec((rows, COL), lambda i: (i, 1)),
            pl.BlockSpec((rows, COL), lambda i: (i, 0)),
            pl.BlockSpec((rows, COL), lambda i: (i, 2)),
            pl.BlockSpec((rows, COL), lambda i: (i, 3)),
            pl.BlockSpec((1, COL), lambda i: (0, 0)),
            pl.BlockSpec((SLAB, SLAB), lambda i: (0, 0)),
        ],
        out_specs=pl.BlockSpec((rows, COL), lambda i: (i, 0)),
        out_shape=jax.ShapeDtypeStruct((T, COL), BF16),
        scratch_shapes=[pltpu.VMEM((HGRN_HEADS, HGRN_DIM, HGRN_DIM), F32)],
        compiler_params=_cparams(("arbitrary",)),
        name="hgrn",
    )(zc, f, zc, zc, ng, l16)


_R_E1, _R_E2, _R_W1, _R_W2, _R_RANK1, _R_RANK2 = range(6)
_GROUP_LANE0 = N_EXPERTS


def _merge_kernel(ya_ref, yb_ref, ga0_ref, ga1_ref, gb0_ref, gb1_ref, x_ref,
                  wa_ref, wb_ref, wo_ref, g_ref, wr_ref, br_ref, tri_ref,
                  x1_ref, h2_ref, route_ref, cnt_ref, cnt_scr):
    step = pl.program_id(0)

    @pl.when(step == 0)
    def _():
        cnt_scr[...] = jnp.zeros_like(cnt_scr)

    pa = jnp.dot(ya_ref[...], wa_ref[...], preferred_element_type=F32)
    pb = jnp.dot(yb_ref[...], wb_ref[...], preferred_element_type=F32)
    ga = jnp.concatenate([ga0_ref[...], ga1_ref[...]], axis=1).astype(F32)
    gb = jnp.concatenate([gb0_ref[...], gb1_ref[...]], axis=1).astype(F32)
    merged = ga * pa + gb * pb
    x1 = x_ref[...] + jnp.dot(merged.astype(BF16), wo_ref[...], preferred_element_type=F32)
    x1_ref[...] = x1
    ms = jnp.mean(x1 * x1, axis=-1, keepdims=True)
    h2 = x1 * lax.rsqrt(ms + EPS) * g_ref[...]
    h2_ref[...] = h2
    logits = jnp.dot(h2.astype(BF16), wr_ref[...], preferred_element_type=F32) + br_ref[...]

    tm = logits.shape[0]
    lane = lax.broadcasted_iota(jnp.int32, (tm, LANES), 1).astype(F32)
    neg = jnp.float32(-jnp.inf)
    big = jnp.float32(1e9)
    is_group = (lane >= _GROUP_LANE0) & (lane < _GROUP_LANE0 + N_GROUPS)
    gl = jnp.where(is_group, logits, neg)
    gmax = jnp.max(gl, axis=-1, keepdims=True)
    g_sel = jnp.min(jnp.where(gl == gmax, lane, big), axis=-1, keepdims=True) - _GROUP_LANE0
    p_g = 1.0 / jnp.sum(jnp.exp(gl - gmax), axis=-1, keepdims=True)
    lo = g_sel * EXPERTS_PER_GROUP
    in_group = (lane >= lo) & (lane < lo + EXPERTS_PER_GROUP)
    el = jnp.where(in_group, logits, neg)
    m1 = jnp.max(el, axis=-1, keepdims=True)
    e1 = jnp.min(jnp.where(el == m1, lane, big), axis=-1, keepdims=True)
    el2 = jnp.where(lane == e1, neg, el)
    m2 = jnp.max(el2, axis=-1, keepdims=True)
    e2 = jnp.min(jnp.where(el2 == m2, lane, big), axis=-1, keepdims=True)
    ex = jnp.exp(m2 - m1)
    w1 = p_g / (1.0 + ex)
    w2 = p_g * ex / (1.0 + ex)

    oh1 = (lane == e1).astype(F32)
    oh2 = (lane == e2).astype(F32)
    both = oh1 + oh2
    before = jnp.dot(tri_ref[...], both.astype(BF16), preferred_element_type=F32) + cnt_scr[...]
    rank1 = jnp.sum(oh1 * before, axis=-1, keepdims=True)
    rank2 = jnp.sum(oh2 * before, axis=-1, keepdims=True)
    cnt = cnt_scr[...] + jnp.sum(both, axis=0, keepdims=True)
    cnt_scr[...] = cnt
    cnt_ref[...] = cnt

    rec = jnp.zeros((tm, LANES), F32)
    for idx, val in ((_R_E1, e1), (_R_E2, e2), (_R_W1, w1), (_R_W2, w2),
                     (_R_RANK1, rank1), (_R_RANK2, rank2)):
        rec = jnp.where(lane == idx, val, rec)
    route_ref[...] = rec


def _merge(zc, yb, x2, wa, wb, wo, g, wr, br, tri, tm):
    T = x2.shape[0]
    const = lambda shape: pl.BlockSpec(shape, lambda i: (0,) * len(shape),
                                       pipeline_mode=pl.Buffered(1))
    return pl.pallas_call(
        _merge_kernel,
        grid=(T // tm,),
        in_specs=[
            pl.BlockSpec((tm, COL), lambda i: (i, 0)),
            pl.BlockSpec((tm, COL), lambda i: (i, 0)),
            pl.BlockSpec((tm, COL), lambda i: (i, 4)),
            pl.BlockSpec((tm, COL), lambda i: (i, 5)),
            pl.BlockSpec((tm, COL), lambda i: (i, 6)),
            pl.BlockSpec((tm, COL), lambda i: (i, 7)),
            pl.BlockSpec((tm, D_MODEL), lambda i: (i, 0)),
            const((GMLP_WIDTH, D_MODEL)),
            const((HGRN_WIDTH, D_MODEL)),
            const((D_MODEL, D_MODEL)),
            const((1, D_MODEL)),
            const((D_MODEL, LANES)),
            const((1, LANES)),
            const((tm, tm)),
        ],
        out_specs=[
            pl.BlockSpec((tm, D_MODEL), lambda i: (i, 0)),
            pl.BlockSpec((tm, D_MODEL), lambda i: (i, 0)),
            pl.BlockSpec((tm, LANES), lambda i: (i, 0)),
            pl.BlockSpec((1, LANES), lambda i: (0, 0)),
        ],
        out_shape=[
            jax.ShapeDtypeStruct((T, D_MODEL), F32),
            jax.ShapeDtypeStruct((T, D_MODEL), F32),
            jax.ShapeDtypeStruct((T, LANES), F32),
            jax.ShapeDtypeStruct((1, LANES), F32),
        ],
        scratch_shapes=[pltpu.VMEM((1, LANES), F32)],
        compiler_params=_cparams(("arbitrary",)),
        name="merge_route",
    )(zc, yb, zc, zc, zc, zc, x2, wa, wb, wo, g, wr, br, tri)


def _dispatch_kernel(dest_ref, h_ref, xb_in_ref, xb_ref, sem):
    del xb_in_ref
    tb = h_ref.shape[0]

    def row_copy(t, k):
        d = dest_ref[0, 0, 2 * t + k]
        return pltpu.make_async_copy(h_ref.at[pl.ds(t, 1)], xb_ref.at[pl.ds(d, 1)], sem)

    def start(t, c):
        row_copy(t, 0).start()
        row_copy(t, 1).start()
        return c

    def wait(t, c):
        row_copy(t, 0).wait()
        row_copy(t, 1).wait()
        return c

    lax.fori_loop(0, tb, start, 0)
    lax.fori_loop(0, tb, wait, 0)


def _dispatch(dest3, h2, xb0, tb):
    T = h2.shape[0]
    return pl.pallas_call(
        _dispatch_kernel,
        grid=(T // tb,),
        in_specs=[
            pl.BlockSpec((1, 1, 2 * tb), lambda i: (i, 0, 0), memory_space=pltpu.SMEM),
            pl.BlockSpec((tb, D_MODEL), lambda i: (i, 0)),
            pl.BlockSpec(memory_space=pl.ANY),
        ],
        out_specs=pl.BlockSpec(memory_space=pl.ANY),
        out_shape=jax.ShapeDtypeStruct(xb0.shape, xb0.dtype),
        scratch_shapes=[pltpu.SemaphoreType.DMA(())],
        input_output_aliases={2: 0},
        compiler_params=_cparams(("arbitrary",)),
        name="dispatch",
    )(dest3, h2, xb0)


def _moe_kernel(be_ref, nused_ref, x_ref, wg_ref, wu_ref, wd_ref, y_ref):
    del be_ref
    used = pl.program_id(0) < nused_ref[0]

    @pl.when(used)
    def _():
        x = x_ref[...].astype(BF16)
        a = jnp.dot(x, wg_ref[0], preferred_element_type=F32)
        b = jnp.dot(x, wu_ref[0], preferred_element_type=F32)
        hdn = (jax.nn.silu(a) * b).astype(BF16)
        y_ref[...] = jnp.dot(hdn, wd_ref[0], preferred_element_type=F32)

    @pl.when(jnp.logical_not(used))
    def _():
        y_ref[...] = jnp.zeros_like(y_ref)


def _moe(block_e, n_used, xb, wg, wu, wd):
    P = xb.shape[0]
    n_blocks = P // MOE_BLOCK

    def blk(b, be, nu):
        return jnp.minimum(b, nu[0] - 1)

    grid_spec = pltpu.PrefetchScalarGridSpec(
        num_scalar_prefetch=2,
        grid=(n_blocks,),
        in_specs=[
            pl.BlockSpec((MOE_BLOCK, D_MODEL), lambda b, be, nu: (blk(b, be, nu), 0)),
            pl.BlockSpec((1, D_MODEL, D_EXPERT), lambda b, be, nu: (be[blk(b, be, nu)], 0, 0)),
            pl.BlockSpec((1, D_MODEL, D_EXPERT), lambda b, be, nu: (be[blk(b, be, nu)], 0, 0)),
            pl.BlockSpec((1, D_EXPERT, D_MODEL), lambda b, be, nu: (be[blk(b, be, nu)], 0, 0)),
        ],
        out_specs=pl.BlockSpec((MOE_BLOCK, D_MODEL), lambda b, be, nu: (b, 0)),
    )
    return pl.pallas_call(
        _moe_kernel,
        grid_spec=grid_spec,
        out_shape=jax.ShapeDtypeStruct((P, D_MODEL), F32),
        compiler_params=_cparams(("arbitrary",)),
        name="moe_experts",
    )(block_e, n_used, xb, wg, wu, wd)


def _combine_kernel(dest_ref, yb_ref, x1_ref, route_ref, g_ref, out_ref, buf, sem):
    tb = x1_ref.shape[0]

    def row_copy(t, k):
        d = dest_ref[0, 0, 2 * t + k]
        return pltpu.make_async_copy(yb_ref.at[pl.ds(d, 1)], buf.at[k, pl.ds(t, 1)], sem)

    def start(t, c):
        row_copy(t, 0).start()
        row_copy(t, 1).start()
        return c

    def wait(t, c):
        row_copy(t, 0).wait()
        row_copy(t, 1).wait()
        return c

    lax.fori_loop(0, tb, start, 0)
    lax.fori_loop(0, tb, wait, 0)

    rec = route_ref[...]
    w1 = rec[:, _R_W1:_R_W1 + 1]
    w2 = rec[:, _R_W2:_R_W2 + 1]
    x2 = x1_ref[...] + (buf[0] * w1 + buf[1] * w2)
    ms = jnp.mean(x2 * x2, axis=-1, keepdims=True)
    out_ref[...] = x2 * lax.rsqrt(ms + EPS) * g_ref[...]


def _combine(dest3, yb, x1, route, g, tb):
    T = x1.shape[0]
    return pl.pallas_call(
        _combine_kernel,
        grid=(T // tb,),
        in_specs=[
            pl.BlockSpec((1, 1, 2 * tb), lambda i: (i, 0, 0), memory_space=pltpu.SMEM),
            pl.BlockSpec(memory_space=pl.ANY),
            pl.BlockSpec((tb, D_MODEL), lambda i: (i, 0)),
            pl.BlockSpec((tb, LANES), lambda i: (i, 0)),
            pl.BlockSpec((1, D_MODEL), lambda i: (0, 0)),
        ],
        out_specs=pl.BlockSpec((tb, D_MODEL), lambda i: (i, 0)),
        out_shape=jax.ShapeDtypeStruct((T, D_MODEL), F32),
        scratch_shapes=[pltpu.VMEM((2, tb, D_MODEL), F32), pltpu.SemaphoreType.DMA(())],
        compiler_params=_cparams(("arbitrary",)),
        name="combine",
    )(dest3, yb, x1, route, g)


def _tile(T, pref):
    return pref if T % pref == 0 else T


def kernel(x, norm_mix_g, w_in, gmlp_ln_g, gmlp_ln_b, w_spatial, b_spatial, hgrn_lb_logits,
           hgrn_norm_g, w_branch_a, w_branch_b, w_out, norm_ffn_g, w_router_group, b_router_group,
           w_router_expert, b_router_expert, w_expert_gate, w_expert_up, w_expert_down, norm_final_g):
    bn, s, d = x.shape
    T = bn * s
    x2 = x.reshape(T, d)
    l = 0

    lb = jnp.cumsum(jax.nn.softmax(hgrn_lb_logits.astype(F32), axis=0), axis=0)[l][None, :]
    causal = jnp.tril(jnp.ones((GMLP_CHUNK, GMLP_CHUNK), dtype=bool))
    ws_bf = jnp.where(causal[None], w_spatial[l], 0.0).astype(BF16)
    bs_full = jnp.repeat(b_spatial[l].T, GMLP_WIDTH // GMLP_GROUPS, axis=1)
    r = np.arange(SLAB)
    l16 = jnp.asarray(((r[:, None] // HGRN_CHUNK == r[None, :] // HGRN_CHUNK)
                       & (r[None, :] <= r[:, None])).astype(np.float32), dtype=BF16)
    wr = jnp.concatenate([w_router_expert[l], w_router_group[l],
                          jnp.zeros((d, LANES - N_EXPERTS - N_GROUPS), F32)], axis=1).astype(BF16)
    br = jnp.concatenate([b_router_expert[l], b_router_group[l],
                          jnp.zeros((LANES - N_EXPERTS - N_GROUPS,), F32)])[None, :].astype(F32)

    tm1 = _tile(T, 512)
    zc, f = _inproj(x2, norm_mix_g[l][None, :], w_in[l].astype(BF16), gmlp_ln_g[l][None, :],
                    gmlp_ln_b[l][None, :], ws_bf, bs_full, lb, tm1)

    y_b = _hgrn(zc, f, hgrn_norm_g[l][None, :], l16, _tile(T, 256))

    tm3 = _tile(T, 256)
    rr = np.arange(tm3)
    tri = jnp.asarray((rr[None, :] < rr[:, None]).astype(np.float32), dtype=BF16)
    x1, h2, route, counts = _merge(
        zc, y_b, x2, w_branch_a[l].astype(BF16), w_branch_b[l].astype(BF16), w_out[l].astype(BF16),
        norm_ffn_g[l][None, :], wr, br, tri, tm3)

    counts = counts[0, :N_EXPERTS].astype(jnp.int32)
    padded = ((counts + MOE_BLOCK - 1) // MOE_BLOCK) * MOE_BLOCK
    pad_end = jnp.cumsum(padded)
    pad_start = pad_end - padded
    n_assign = 2 * T
    n_blocks = (n_assign + N_EXPERTS * (MOE_BLOCK - 1) + MOE_BLOCK - 1) // MOE_BLOCK
    block_start = jnp.arange(n_blocks, dtype=jnp.int32) * MOE_BLOCK
    block_e = jnp.minimum(jnp.sum(pad_end[None, :] <= block_start[:, None], axis=1),
                          N_EXPERTS - 1).astype(jnp.int32)
    n_used = (pad_end[-1:] // MOE_BLOCK).astype(jnp.int32)
    eid = route[:, _R_E1:_R_E2 + 1].astype(jnp.int32)
    rank = route[:, _R_RANK1:_R_RANK2 + 1].astype(jnp.int32)
    dest = pad_start[eid] + rank

    tb = _tile(T, 256)
    dest3 = dest.reshape(T // tb, 1, 2 * tb)
    xb = _dispatch(dest3, h2, jnp.zeros((n_blocks * MOE_BLOCK, d), F32), tb)
    yb = _moe(block_e, n_used, xb, w_expert_gate[l].astype(BF16), w_expert_up[l].astype(BF16),
              w_expert_down[l].astype(BF16))
    out = _combine(dest3, yb, x1, route, norm_final_g[None, :], tb)
    return out.reshape(bn, s, d)
```

```python
import functools

import jax
import jax.numpy as jnp
import numpy as np
from jax import lax
from jax.experimental import pallas as pl
from jax.experimental.pallas import tpu as pltpu

F32 = jnp.float32
BF16 = jnp.bfloat16

D_MODEL = 2048
EPS = 1e-6
GMLP_WIDTH = 1024
GMLP_GROUPS = 8
GMLP_CHUNK = 128
HGRN_HEADS = 8
HGRN_DIM = 128
HGRN_WIDTH = HGRN_HEADS * HGRN_DIM
HGRN_CHUNK = 16
SLAB = 128
N_GROUPS = 8
EXPERTS_PER_GROUP = 8
N_EXPERTS = N_GROUPS * EXPERTS_PER_GROUP
D_EXPERT = 1024
MOE_BLOCK = 128
COL = 1024
N_COL_STEPS = 10
LANES = 128

VMEM_LIMIT = 56 * 1024 * 1024


def _cparams(sem):
    return pltpu.CompilerParams(dimension_semantics=sem, vmem_limit_bytes=VMEM_LIMIT)


def _inproj_kernel(x_ref, g_ref, w_ref, lng_ref, lnb_ref, ws_ref, bs_ref, lb_ref,
                   zc_ref, f_ref, h_scr, u_scr, v_scr):
    j = pl.program_id(1)
    tm = x_ref.shape[0]

    @pl.when(j == 0)
    def _():
        x = x_ref[...]
        ms = jnp.mean(x * x, axis=-1, keepdims=True)
        h_scr[...] = (x * lax.rsqrt(ms + EPS) * g_ref[...]).astype(BF16)

    acc = jnp.dot(h_scr[...], w_ref[...], preferred_element_type=F32)

    @pl.when(j == 0)
    def _():
        u_scr[...] = jax.nn.gelu(acc)

    @pl.when(j == 1)
    def _():
        gv = jax.nn.gelu(acc)
        mu = jnp.mean(gv, axis=-1, keepdims=True)
        var = jnp.mean(jnp.square(gv - mu), axis=-1, keepdims=True)
        vn = (gv - mu) * lax.rsqrt(var + EPS) * lng_ref[...] + lnb_ref[...]
        v_scr[...] = vn.astype(BF16)
        for c in range(tm // GMLP_CHUNK):
            rows = slice(c * GMLP_CHUNK, (c + 1) * GMLP_CHUNK)
            for g in range(GMLP_GROUPS):
                cols = slice(g * LANES, (g + 1) * LANES)
                sv = jnp.dot(ws_ref[g], v_scr[rows, cols], preferred_element_type=F32)
                sv = sv + bs_ref[:, cols]
                zc_ref[rows, cols] = (u_scr[rows, cols] * sv).astype(BF16)

    @pl.when(j == 2)
    def _():
        zc_ref[...] = acc.astype(BF16)

    @pl.when(j == 3)
    def _():
        lb = lb_ref[...]
        f_ref[...] = lb + (1.0 - lb) * jax.nn.sigmoid(acc)

    @pl.when(j == 4)
    def _():
        zc_ref[...] = acc.astype(BF16)

    @pl.when(j == 5)
    def _():
        zc_ref[...] = jax.nn.silu(acc).astype(BF16)

    @pl.when(j >= 6)
    def _():
        zc_ref[...] = jax.nn.sigmoid(acc).astype(BF16)


def _zc_col(j):
    return jnp.where(j < 2, 0, jnp.where(j < 4, 1, j - 2))


def _inproj(x2, g, w_bf, lng, lnb, ws_bf, bs_full, lb, tm):
    T = x2.shape[0]
    grid = (T // tm, N_COL_STEPS)
    return pl.pallas_call(
        _inproj_kernel,
        grid=grid,
        in_specs=[
            pl.BlockSpec((tm, D_MODEL), lambda i, j: (i, 0)),
            pl.BlockSpec((1, D_MODEL), lambda i, j: (0, 0)),
            pl.BlockSpec((D_MODEL, COL), lambda i, j: (0, j)),
            pl.BlockSpec((1, COL), lambda i, j: (0, 0)),
            pl.BlockSpec((1, COL), lambda i, j: (0, 0)),
            pl.BlockSpec((GMLP_GROUPS, GMLP_CHUNK, GMLP_CHUNK), lambda i, j: (0, 0, 0)),
            pl.BlockSpec((GMLP_CHUNK, COL), lambda i, j: (0, 0)),
            pl.BlockSpec((1, COL), lambda i, j: (0, 0)),
        ],
        out_specs=[
            pl.BlockSpec((tm, COL), lambda i, j: (i, _zc_col(j))),
            pl.BlockSpec((tm, COL), lambda i, j: (i, 0)),
        ],
        out_shape=[
            jax.ShapeDtypeStruct((T, 8 * COL), BF16),
            jax.ShapeDtypeStruct((T, COL), F32),
        ],
        scratch_shapes=[
            pltpu.VMEM((tm, D_MODEL), BF16),
            pltpu.VMEM((tm, COL), F32),
            pltpu.VMEM((tm, COL), BF16),
        ],
        compiler_params=_cparams(("arbitrary", "arbitrary")),
        name="inproj",
    )(x2, g, w_bf, lng, lnb, ws_bf, bs_full, lb)


_N_CHUNKS = SLAB // HGRN_CHUNK


def _hgrn_kernel(q_ref, f_ref, i_ref, sg_ref, ng_ref, l16_ref, o_ref, st_scr):
    step = pl.program_id(0)

    @pl.when(step == 0)
    def _():
        st_scr[...] = jnp.zeros_like(st_scr)

    rows_per_step = q_ref.shape[0]
    row = lax.broadcasted_iota(jnp.int32, (SLAB, SLAB), 0)
    col = lax.broadcasted_iota(jnp.int32, (SLAB, SLAB), 1)
    rc = row // HGRN_CHUNK
    cc = col // HGRN_CHUNK
    masks = [(rc == cc) & (col <= row)]
    for nb in (1, 2, 4):
        rs = rc // nb
        cs = cc // nb
        masks.append(((rs % 2) == 1) & (cs == rs - 1))

    for sl in range(rows_per_step // SLAB):
        r0 = sl * SLAB
        f = f_ref[r0:r0 + SLAB, :]
        lf = jnp.log(f)
        kk = 1.0 - f
        lf_hi = lf.astype(BF16)
        lf_lo = (lf - lf_hi.astype(F32)).astype(BF16)
        l16 = l16_ref[...]
        e16 = (jnp.dot(l16, lf_hi, preferred_element_type=F32)
               + jnp.dot(l16, lf_lo, preferred_element_type=F32))
        tot = [e16[c * HGRN_CHUNK + HGRN_CHUNK - 1:c * HGRN_CHUNK + HGRN_CHUNK, :]
               for c in range(_N_CHUNKS)]
        r16 = jnp.concatenate(
            [tot[c] - e16[c * HGRN_CHUNK:(c + 1) * HGRN_CHUNK, :] for c in range(_N_CHUNKS)], axis=0)
        qf = q_ref[r0:r0 + SLAB, :].astype(F32)
        q_dec = qf * jnp.exp(e16)
        k_inv = kk * jnp.exp(-e16)
        k_dec = kk * jnp.exp(r16)

        def chunk_sum(lo, hi):
            acc = None
            for c in range(lo, hi):
                acc = tot[c] if acc is None else acc + tot[c]
            return acc

        def scaled(base, offsets):
            parts = []
            for c in range(_N_CHUNKS):
                blk = base[c * HGRN_CHUNK:(c + 1) * HGRN_CHUNK, :]
                if offsets[c] is not None:
                    blk = blk * jnp.exp(offsets[c])
                parts.append(blk)
            return jnp.concatenate(parts, axis=0)

        q_lv = [q_dec, q_dec]
        k_lv = [k_inv, k_dec]
        for nb in (2, 4, 8):
            q_off = [chunk_sum((c // nb) * nb, c) for c in range(_N_CHUNKS)]
            k_off = [chunk_sum(c + 1, (c // nb + 1) * nb) for c in range(_N_CHUNKS)]
            q_lv.append(scaled(q_dec, q_off))
            k_lv.append(scaled(k_dec, k_off))
        slab_decay = jnp.exp(chunk_sum(0, _N_CHUNKS))

        q_bf = [a.astype(BF16) for a in q_lv]
        k_bf = [a.astype(BF16) for a in k_lv]
        i_bf = i_ref[r0:r0 + SLAB, :]

        for h in range(HGRN_HEADS):
            cols = slice(h * HGRN_DIM, (h + 1) * HGRN_DIM)
            scores = None
            for lv in range(4):
                s_lv = lax.dot_general(q_bf[lv][:, cols], k_bf[lv][:, cols],
                                       (((1,), (1,)), ((), ())), preferred_element_type=F32)
                s_lv = jnp.where(masks[lv], s_lv, 0.0)
                scores = s_lv if scores is None else scores + s_lv
            ih = i_bf[:, cols]
            st = st_scr[h]
            o = jnp.dot(scores.astype(BF16), ih, preferred_element_type=F32)
            o = o + lax.dot_general(q_bf[4][:, cols], st.astype(BF16),
                                    (((1,), (1,)), ((), ())), preferred_element_type=F32)
            upd = lax.dot_general(ih, k_bf[4][:, cols],
                                  (((0,), (0,)), ((), ())), preferred_element_type=F32)
            st_scr[h] = st * slab_decay[:, cols] + upd
            ms = jnp.mean(o * o, axis=-1, keepdims=True)
            on = o * lax.rsqrt(ms + EPS) * ng_ref[:, cols]
            o_ref[r0:r0 + SLAB, cols] = (on * sg_ref[r0:r0 + SLAB, cols].astype(F32)).astype(BF16)


def _hgrn(zc, f, ng, l16, rows):
    T = f.shape[0]
    return pl.pallas_call(
        _hgrn_kernel,
        grid=(T // rows,),
        in_specs=[
            pl.BlockSpec((rows, COL), lambda i: (i, 1)),
            pl.BlockSpec((rows, COL), lambda i: (i, 0)),
            pl.BlockSpec((rows, COL), lambda i: (i, 2)),
            pl.BlockSpec((rows, COL), lambda i: (i, 3)),
            pl.BlockSpec((1, COL), lambda i: (0, 0)),
            pl.BlockSpec((SLAB, SLAB), lambda i: (0, 0)),
        ],
        out_specs=pl.BlockSpec((rows, COL), lambda i: (i, 0)),
        out_shape=jax.ShapeDtypeStruct((T, COL), BF16),
        scratch_shapes=[pltpu.VMEM((HGRN_HEADS, HGRN_DIM, HGRN_DIM), F32)],
        compiler_params=_cparams(("arbitrary",)),
        name="hgrn",
    )(zc, f, zc, zc, ng, l16)


_R_E1, _R_E2, _R_W1, _R_W2, _R_RANK1, _R_RANK2 = range(6)
_GROUP_LANE0 = N_EXPERTS


def _merge_kernel(ya_ref, yb_ref, ga0_ref, ga1_ref, gb0_ref, gb1_ref, x_ref,
                  wa_ref, wb_ref, wo_ref, g_ref, wr_ref, br_ref, tri_ref,
                  x1_ref, h2_ref, route_ref, cnt_ref, cnt_scr):
    step = pl.program_id(0)

    @pl.when(step == 0)
    def _():
        cnt_scr[...] = jnp.zeros_like(cnt_scr)

    pa = jnp.dot(ya_ref[...], wa_ref[...], preferred_element_type=F32)
    pb = jnp.dot(yb_ref[...], wb_ref[...], preferred_element_type=F32)
    ga = jnp.concatenate([ga0_ref[...], ga1_ref[...]], axis=1).astype(F32)
    gb = jnp.concatenate([gb0_ref[...], gb1_ref[...]], axis=1).astype(F32)
    merged = ga * pa + gb * pb
    x1 = x_ref[...] + jnp.dot(merged.astype(BF16), wo_ref[...], preferred_element_type=F32)
    x1_ref[...] = x1
    ms = jnp.mean(x1 * x1, axis=-1, keepdims=True)
    h2 = x1 * lax.rsqrt(ms + EPS) * g_ref[...]
    h2_ref[...] = h2
    logits = jnp.dot(h2.astype(BF16), wr_ref[...], preferred_element_type=F32) + br_ref[...]

    tm = logits.shape[0]
    lane = lax.broadcasted_iota(jnp.int32, (tm, LANES), 1).astype(F32)
    neg = jnp.float32(-jnp.inf)
    big = jnp.float32(1e9)
    is_group = (lane >= _GROUP_LANE0) & (lane < _GROUP_LANE0 + N_GROUPS)
    gl = jnp.where(is_group, logits, neg)
    gmax = jnp.max(gl, axis=-1, keepdims=True)
    g_sel = jnp.min(jnp.where(gl == gmax, lane, big), axis=-1, keepdims=True) - _GROUP_LANE0
    p_g = 1.0 / jnp.sum(jnp.exp(gl - gmax), axis=-1, keepdims=True)
    lo = g_sel * EXPERTS_PER_GROUP
    in_group = (lane >= lo) & (lane < lo + EXPERTS_PER_GROUP)
    el = jnp.where(in_group, logits, neg)
    m1 = jnp.max(el, axis=-1, keepdims=True)
    e1 = jnp.min(jnp.where(el == m1, lane, big), axis=-1, keepdims=True)
    el2 = jnp.where(lane == e1, neg, el)
    m2 = jnp.max(el2, axis=-1, keepdims=True)
    e2 = jnp.min(jnp.where(el2 == m2, lane, big), axis=-1, keepdims=True)
    ex = jnp.exp(m2 - m1)
    w1 = p_g / (1.0 + ex)
    w2 = p_g * ex / (1.0 + ex)

    oh1 = (lane == e1).astype(F32)
    oh2 = (lane == e2).astype(F32)
    both = oh1 + oh2
    before = jnp.dot(tri_ref[...], both.astype(BF16), preferred_element_type=F32) + cnt_scr[...]
    rank1 = jnp.sum(oh1 * before, axis=-1, keepdims=True)
    rank2 = jnp.sum(oh2 * before, axis=-1, keepdims=True)
    cnt = cnt_scr[...] + jnp.sum(both, axis=0, keepdims=True)
    cnt_scr[...] = cnt
    cnt_ref[...] = cnt

    rec = jnp.zeros((tm, LANES), F32)
    for idx, val in ((_R_E1, e1), (_R_E2, e2), (_R_W1, w1), (_R_W2, w2),
                     (_R_RANK1, rank1), (_R_RANK2, rank2)):
        rec = jnp.where(lane == idx, val, rec)
    route_ref[...] = rec


def _merge(zc, yb, x2, wa, wb, wo, g, wr, br, tri, tm):
    T = x2.shape[0]
    const = lambda shape: pl.BlockSpec(shape, lambda i: (0,) * len(shape),
                                       pipeline_mode=pl.Buffered(1))
    return pl.pallas_call(
        _merge_kernel,
        grid=(T // tm,),
        in_specs=[
            pl.BlockSpec((tm, COL), lambda i: (i, 0)),
            pl.BlockSpec((tm, COL), lambda i: (i, 0)),
            pl.BlockSpec((tm, COL), lambda i: (i, 4)),
            pl.BlockSpec((tm, COL), lambda i: (i, 5)),
            pl.BlockSpec((tm, COL), lambda i: (i, 6)),
            pl.BlockSpec((tm, COL), lambda i: (i, 7)),
            pl.BlockSpec((tm, D_MODEL), lambda i: (i, 0)),
            const((GMLP_WIDTH, D_MODEL)),
            const((HGRN_WIDTH, D_MODEL)),
            const((D_MODEL, D_MODEL)),
            const((1, D_MODEL)),
            const((D_MODEL, LANES)),
            const((1, LANES)),
            const((tm, tm)),
        ],
        out_specs=[
            pl.BlockSpec((tm, D_MODEL), lambda i: (i, 0)),
            pl.BlockSpec((tm, D_MODEL), lambda i: (i, 0)),
            pl.BlockSpec((tm, LANES), lambda i: (i, 0)),
            pl.BlockSpec((1, LANES), lambda i: (0, 0)),
        ],
        out_shape=[
            jax.ShapeDtypeStruct((T, D_MODEL), F32),
            jax.ShapeDtypeStruct((T, D_MODEL), F32),
            jax.ShapeDtypeStruct((T, LANES), F32),
            jax.ShapeDtypeStruct((1, LANES), F32),
        ],
        scratch_shapes=[pltpu.VMEM((1, LANES), F32)],
        compiler_params=_cparams(("arbitrary",)),
        name="merge_route",
    )(zc, yb, zc, zc, zc, zc, x2, wa, wb, wo, g, wr, br, tri)


def _dispatch_kernel(dest_ref, h_ref, xb_in_ref, xb_ref, sem):
    del xb_in_ref
    tb = h_ref.shape[0]

    def row_copy(t, k):
        d = dest_ref[0, 0, 2 * t + k]
        return pltpu.make_async_copy(h_ref.at[pl.ds(t, 1)], xb_ref.at[pl.ds(d, 1)], sem)

    def start(t, c):
        row_copy(t, 0).start()
        row_copy(t, 1).start()
        return c

    def wait(t, c):
        row_copy(t, 0).wait()
        row_copy(t, 1).wait()
        return c

    lax.fori_loop(0, tb, start, 0)
    lax.fori_loop(0, tb, wait, 0)


def _dispatch(dest3, h2, xb0, tb):
    T = h2.shape[0]
    return pl.pallas_call(
        _dispatch_kernel,
        grid=(T // tb,),
        in_specs=[
            pl.BlockSpec((1, 1, 2 * tb), lambda i: (i, 0, 0), memory_space=pltpu.SMEM),
            pl.BlockSpec((tb, D_MODEL), lambda i: (i, 0)),
            pl.BlockSpec(memory_space=pl.ANY),
        ],
        out_specs=pl.BlockSpec(memory_space=pl.ANY),
        out_shape=jax.ShapeDtypeStruct(xb0.shape, xb0.dtype),
        scratch_shapes=[pltpu.SemaphoreType.DMA(())],
        input_output_aliases={2: 0},
        compiler_params=_cparams(("arbitrary",)),
        name="dispatch",
    )(dest3, h2, xb0)


def _moe_kernel(be_ref, first_ref, nexte_ref, nused_ref, x_ref, wg_hbm, wu_hbm, wd_hbm, y_ref,
                stg_g, stg_u, stg_d, wg_bf, wu_bf, wd_bf, sems):
    b = pl.program_id(0)
    used = b < nused_ref[0]

    def weight_copies(e):
        return (pltpu.make_async_copy(wg_hbm.at[e], stg_g, sems.at[0]),
                pltpu.make_async_copy(wu_hbm.at[e], stg_u, sems.at[1]),
                pltpu.make_async_copy(wd_hbm.at[e], stg_d, sems.at[2]))

    @pl.when(b == 0)
    def _():
        for cp in weight_copies(be_ref[0]):
            cp.start()

    @pl.when(used & (first_ref[b] == 1))
    def _():
        cg, cu, cd = weight_copies(be_ref[b])
        cg.wait()
        wg_bf[...] = stg_g[...].astype(BF16)
        cu.wait()
        wu_bf[...] = stg_u[...].astype(BF16)
        cd.wait()
        wd_bf[...] = stg_d[...].astype(BF16)
        nxt = nexte_ref[b]

        @pl.when(nxt >= 0)
        def _():
            for cp in weight_copies(nxt):
                cp.start()

    @pl.when(used)
    def _():
        x = x_ref[...].astype(BF16)
        a = jnp.dot(x, wg_bf[...], preferred_element_type=F32)
        g = jnp.dot(x, wu_bf[...], preferred_element_type=F32)
        hdn = (jax.nn.silu(a) * g).astype(BF16)
        y_ref[...] = jnp.dot(hdn, wd_bf[...], preferred_element_type=F32)

    @pl.when(jnp.logical_not(used))
    def _():
        y_ref[...] = jnp.zeros_like(y_ref)


def _moe(block_e, first, next_e, n_used, xb, wg, wu, wd):
    P = xb.shape[0]
    n_blocks = P // MOE_BLOCK
    grid_spec = pltpu.PrefetchScalarGridSpec(
        num_scalar_prefetch=4,
        grid=(n_blocks,),
        in_specs=[
            pl.BlockSpec((MOE_BLOCK, D_MODEL), lambda b, be, fi, ne, nu: (jnp.minimum(b, nu[0] - 1), 0)),
            pl.BlockSpec(memory_space=pl.ANY),
            pl.BlockSpec(memory_space=pl.ANY),
            pl.BlockSpec(memory_space=pl.ANY),
        ],
        out_specs=pl.BlockSpec((MOE_BLOCK, D_MODEL), lambda b, be, fi, ne, nu: (b, 0)),
        scratch_shapes=[
            pltpu.VMEM((D_MODEL, D_EXPERT), F32),
            pltpu.VMEM((D_MODEL, D_EXPERT), F32),
            pltpu.VMEM((D_EXPERT, D_MODEL), F32),
            pltpu.VMEM((D_MODEL, D_EXPERT), BF16),
            pltpu.VMEM((D_MODEL, D_EXPERT), BF16),
            pltpu.VMEM((D_EXPERT, D_MODEL), BF16),
            pltpu.SemaphoreType.DMA((3,)),
        ],
    )
    return pl.pallas_call(
        _moe_kernel,
        grid_spec=grid_spec,
        out_shape=jax.ShapeDtypeStruct((P, D_MODEL), F32),
        compiler_params=_cparams(("arbitrary",)),
        name="moe_experts",
    )(block_e, first, next_e, n_used, xb, wg, wu, wd)


def _combine_kernel(dest_ref, yb_ref, x1_ref, route_ref, g_ref, out_ref, buf, sem):
    tb = x1_ref.shape[0]

    def row_copy(t, k):
        d = dest_ref[0, 0, 2 * t + k]
        return pltpu.make_async_copy(yb_ref.at[pl.ds(d, 1)], buf.at[k, pl.ds(t, 1)], sem)

    def start(t, c):
        row_copy(t, 0).start()
        row_copy(t, 1).start()
        return c

    def wait(t, c):
        row_copy(t, 0).wait()
        row_copy(t, 1).wait()
        return c

    lax.fori_loop(0, tb, start, 0)
    lax.fori_loop(0, tb, wait, 0)

    rec = route_ref[...]
    w1 = rec[:, _R_W1:_R_W1 + 1]
    w2 = rec[:, _R_W2:_R_W2 + 1]
    x2 = x1_ref[...] + (buf[0] * w1 + buf[1] * w2)
    ms = jnp.mean(x2 * x2, axis=-1, keepdims=True)
    out_ref[...] = x2 * lax.rsqrt(ms + EPS) * g_ref[...]


def _combine(dest3, yb, x1, route, g, tb):
    T = x1.shape[0]
    return pl.pallas_call(
        _combine_kernel,
        grid=(T // tb,),
        in_specs=[
            pl.BlockSpec((1, 1, 2 * tb), lambda i: (i, 0, 0), memory_space=pltpu.SMEM),
            pl.BlockSpec(memory_space=pl.ANY),
            pl.BlockSpec((tb, D_MODEL), lambda i: (i, 0)),
            pl.BlockSpec((tb, LANES), lambda i: (i, 0)),
            pl.BlockSpec((1, D_MODEL), lambda i: (0, 0)),
        ],
        out_specs=pl.BlockSpec((tb, D_MODEL), lambda i: (i, 0)),
        out_shape=jax.ShapeDtypeStruct((T, D_MODEL), F32),
        scratch_shapes=[pltpu.VMEM((2, tb, D_MODEL), F32), pltpu.SemaphoreType.DMA(())],
        compiler_params=_cparams(("arbitrary",)),
        name="combine",
    )(dest3, yb, x1, route, g)


def _tile(T, pref):
    return pref if T % pref == 0 else T


def kernel(x, norm_mix_g, w_in, gmlp_ln_g, gmlp_ln_b, w_spatial, b_spatial, hgrn_lb_logits,
           hgrn_norm_g, w_branch_a, w_branch_b, w_out, norm_ffn_g, w_router_group, b_router_group,
           w_router_expert, b_router_expert, w_expert_gate, w_expert_up, w_expert_down, norm_final_g):
    bn, s, d = x.shape
    T = bn * s
    x2 = x.reshape(T, d)
    l = 0

    lb = jnp.cumsum(jax.nn.softmax(hgrn_lb_logits.astype(F32), axis=0), axis=0)[l][None, :]
    causal = jnp.tril(jnp.ones((GMLP_CHUNK, GMLP_CHUNK), dtype=bool))
    ws_bf = jnp.where(causal[None], w_spatial[l], 0.0).astype(BF16)
    bs_full = jnp.repeat(b_spatial[l].T, GMLP_WIDTH // GMLP_GROUPS, axis=1)
    r = np.arange(SLAB)
    l16 = jnp.asarray(((r[:, None] // HGRN_CHUNK == r[None, :] // HGRN_CHUNK)
                       & (r[None, :] <= r[:, None])).astype(np.float32), dtype=BF16)
    wr = jnp.concatenate([w_router_expert[l], w_router_group[l],
                          jnp.zeros((d, LANES - N_EXPERTS - N_GROUPS), F32)], axis=1).astype(BF16)
    br = jnp.concatenate([b_router_expert[l], b_router_group[l],
                          jnp.zeros((LANES - N_EXPERTS - N_GROUPS,), F32)])[None, :].astype(F32)

    tm1 = _tile(T, 512)
    zc, f = _inproj(x2, norm_mix_g[l][None, :], w_in[l].astype(BF16), gmlp_ln_g[l][None, :],
                    gmlp_ln_b[l][None, :], ws_bf, bs_full, lb, tm1)

    y_b = _hgrn(zc, f, hgrn_norm_g[l][None, :], l16, _tile(T, 256))

    tm3 = _tile(T, 256)
    rr = np.arange(tm3)
    tri = jnp.asarray((rr[None, :] < rr[:, None]).astype(np.float32), dtype=BF16)
    x1, h2, route, counts = _merge(
        zc, y_b, x2, w_branch_a[l].astype(BF16), w_branch_b[l].astype(BF16), w_out[l].astype(BF16),
        norm_ffn_g[l][None, :], wr, br, tri, tm3)

    counts = counts[0, :N_EXPERTS].astype(jnp.int32)
    padded = ((counts + MOE_BLOCK - 1) // MOE_BLOCK) * MOE_BLOCK
    pad_end = jnp.cumsum(padded)
    pad_start = pad_end - padded
    n_assign = 2 * T
    n_blocks = (n_assign + N_EXPERTS * (MOE_BLOCK - 1) + MOE_BLOCK - 1) // MOE_BLOCK
    block_start = jnp.arange(n_blocks, dtype=jnp.int32) * MOE_BLOCK
    block_e = jnp.minimum(jnp.sum(pad_end[None, :] <= block_start[:, None], axis=1),
                          N_EXPERTS - 1).astype(jnp.int32)
    n_used = (pad_end[-1:] // MOE_BLOCK).astype(jnp.int32)
    first = jnp.concatenate([jnp.ones((1,), jnp.int32),
                             (block_e[1:] != block_e[:-1]).astype(jnp.int32)])
    e_ids = jnp.arange(N_EXPERTS, dtype=jnp.int32)
    later = lax.cummin(jnp.where(padded > 0, e_ids, N_EXPERTS), axis=0, reverse=True)
    next_of = jnp.concatenate([later[1:], jnp.full((1,), N_EXPERTS, jnp.int32)])
    next_of = jnp.where(next_of >= N_EXPERTS, -1, next_of).astype(jnp.int32)
    next_e = next_of[block_e]
    eid = route[:, _R_E1:_R_E2 + 1].astype(jnp.int32)
    rank = route[:, _R_RANK1:_R_RANK2 + 1].astype(jnp.int32)
    dest = pad_start[eid] + rank

    tb = _tile(T, 256)
    dest3 = dest.reshape(T // tb, 1, 2 * tb)
    xb = _dispatch(dest3, h2, jnp.zeros((n_blocks * MOE_BLOCK, d), F32), tb)
    yb = _moe(block_e, first, next_e, n_used, xb, w_expert_gate[l], w_expert_up[l], w_expert_down[l])
    out = _combine(dest3, yb, x1, route, norm_final_g[None, :], tb)
    return out.reshape(bn, s, d)
```

```python
import functools

import jax
import jax.numpy as jnp
import numpy as np
from jax import lax
from jax.experimental import pallas as pl
from jax.experimental.pallas import tpu as pltpu

F32 = jnp.float32
BF16 = jnp.bfloat16

D_MODEL = 2048
EPS = 1e-6
GMLP_WIDTH = 1024
GMLP_GROUPS = 8
GMLP_CHUNK = 128
HGRN_HEADS = 8
HGRN_DIM = 128
HGRN_WIDTH = HGRN_HEADS * HGRN_DIM
HGRN_CHUNK = 16
SLAB = 128
N_GROUPS = 8
EXPERTS_PER_GROUP = 8
N_EXPERTS = N_GROUPS * EXPERTS_PER_GROUP
D_EXPERT = 1024
MOE_BLOCK = 128
COL = 1024
N_COL_STEPS = 10
LANES = 128

VMEM_LIMIT = 56 * 1024 * 1024


def _cparams(sem):
    return pltpu.CompilerParams(dimension_semantics=sem, vmem_limit_bytes=VMEM_LIMIT)


def _inproj_kernel(x_ref, g_ref, w_ref, lng_ref, lnb_ref, ws_ref, bs_ref, lb_ref,
                   zc_ref, f_ref, h_scr, u_scr, v_scr):
    j = pl.program_id(1)
    tm = x_ref.shape[0]

    @pl.when(j == 0)
    def _():
        x = x_ref[...]
        ms = jnp.mean(x * x, axis=-1, keepdims=True)
        h_scr[...] = (x * lax.rsqrt(ms + EPS) * g_ref[...]).astype(BF16)

    acc = jnp.dot(h_scr[...], w_ref[...], preferred_element_type=F32)

    @pl.when(j == 0)
    def _():
        u_scr[...] = jax.nn.gelu(acc)

    @pl.when(j == 1)
    def _():
        gv = jax.nn.gelu(acc)
        mu = jnp.mean(gv, axis=-1, keepdims=True)
        var = jnp.mean(jnp.square(gv - mu), axis=-1, keepdims=True)
        vn = (gv - mu) * lax.rsqrt(var + EPS) * lng_ref[...] + lnb_ref[...]
        v_scr[...] = vn.astype(BF16)
        for c in range(tm // GMLP_CHUNK):
            rows = slice(c * GMLP_CHUNK, (c + 1) * GMLP_CHUNK)
            for g in range(GMLP_GROUPS):
                cols = slice(g * LANES, (g + 1) * LANES)
                sv = jnp.dot(ws_ref[g], v_scr[rows, cols], preferred_element_type=F32)
                sv = sv + bs_ref[:, cols]
                zc_ref[rows, cols] = (u_scr[rows, cols] * sv).astype(BF16)

    @pl.when(j == 2)
    def _():
        zc_ref[...] = acc.astype(BF16)

    @pl.when(j == 3)
    def _():
        lb = lb_ref[...]
        f_ref[...] = lb + (1.0 - lb) * jax.nn.sigmoid(acc)

    @pl.when(j == 4)
    def _():
        zc_ref[...] = acc.astype(BF16)

    @pl.when(j == 5)
    def _():
        zc_ref[...] = jax.nn.silu(acc).astype(BF16)

    @pl.when(j >= 6)
    def _():
        zc_ref[...] = jax.nn.sigmoid(acc).astype(BF16)


def _zc_col(j):
    return jnp.where(j < 2, 0, jnp.where(j < 4, 1, j - 2))


def _inproj(x2, g, w_bf, lng, lnb, ws_bf, bs_full, lb, tm):
    T = x2.shape[0]
    grid = (T // tm, N_COL_STEPS)
    return pl.pallas_call(
        _inproj_kernel,
        grid=grid,
        in_specs=[
            pl.BlockSpec((tm, D_MODEL), lambda i, j: (i, 0)),
            pl.BlockSpec((1, D_MODEL), lambda i, j: (0, 0)),
            pl.BlockSpec((D_MODEL, COL), lambda i, j: (0, j)),
            pl.BlockSpec((1, COL), lambda i, j: (0, 0)),
            pl.BlockSpec((1, COL), lambda i, j: (0, 0)),
            pl.BlockSpec((GMLP_GROUPS, GMLP_CHUNK, GMLP_CHUNK), lambda i, j: (0, 0, 0)),
            pl.BlockSpec((GMLP_CHUNK, COL), lambda i, j: (0, 0)),
            pl.BlockSpec((1, COL), lambda i, j: (0, 0)),
        ],
        out_specs=[
            pl.BlockSpec((tm, COL), lambda i, j: (i, _zc_col(j))),
            pl.BlockSpec((tm, COL), lambda i, j: (i, 0)),
        ],
        out_shape=[
            jax.ShapeDtypeStruct((T, 8 * COL), BF16),
            jax.ShapeDtypeStruct((T, COL), F32),
        ],
        scratch_shapes=[
            pltpu.VMEM((tm, D_MODEL), BF16),
            pltpu.VMEM((tm, COL), F32),
            pltpu.VMEM((tm, COL), BF16),
        ],
        compiler_params=_cparams(("arbitrary", "arbitrary")),
        name="inproj",
    )(x2, g, w_bf, lng, lnb, ws_bf, bs_full, lb)


_N_CHUNKS = SLAB // HGRN_CHUNK


def _hgrn_kernel(q_ref, f_ref, i_ref, sg_ref, ng_ref, l16_ref, o_ref, st_scr):
    step = pl.program_id(0)

    @pl.when(step == 0)
    def _():
        st_scr[...] = jnp.zeros_like(st_scr)

    rows_per_step = q_ref.shape[0]
    row = lax.broadcasted_iota(jnp.int32, (SLAB, SLAB), 0)
    col = lax.broadcasted_iota(jnp.int32, (SLAB, SLAB), 1)
    rc = row // HGRN_CHUNK
    cc = col // HGRN_CHUNK
    masks = [(rc == cc) & (col <= row)]
    for nb in (1, 2, 4):
        rs = rc // nb
        cs = cc // nb
        masks.append(((rs % 2) == 1) & (cs == rs - 1))

    for sl in range(rows_per_step // SLAB):
        r0 = sl * SLAB
        f = f_ref[r0:r0 + SLAB, :]
        lf = jnp.log(f)
        kk = 1.0 - f
        lf_hi = lf.astype(BF16)
        lf_lo = (lf - lf_hi.astype(F32)).astype(BF16)
        l16 = l16_ref[...]
        e16 = (jnp.dot(l16, lf_hi, preferred_element_type=F32)
               + jnp.dot(l16, lf_lo, preferred_element_type=F32))
        tot = [e16[c * HGRN_CHUNK + HGRN_CHUNK - 1:c * HGRN_CHUNK + HGRN_CHUNK, :]
               for c in range(_N_CHUNKS)]
        r16 = jnp.concatenate(
            [tot[c] - e16[c * HGRN_CHUNK:(c + 1) * HGRN_CHUNK, :] for c in range(_N_CHUNKS)], axis=0)
        qf = q_ref[r0:r0 + SLAB, :].astype(F32)
        q_dec = qf * jnp.exp(e16)
        k_inv = kk * jnp.exp(-e16)
        k_dec = kk * jnp.exp(r16)

        def chunk_sum(lo, hi):
            acc = None
            for c in range(lo, hi):
                acc = tot[c] if acc is None else acc + tot[c]
            return acc

        def scaled(base, offsets):
            parts = []
            for c in range(_N_CHUNKS):
                blk = base[c * HGRN_CHUNK:(c + 1) * HGRN_CHUNK, :]
                if offsets[c] is not None:
                    blk = blk * jnp.exp(offsets[c])
                parts.append(blk)
            return jnp.concatenate(parts, axis=0)

        q_lv = [q_dec, q_dec]
        k_lv = [k_inv, k_dec]
        for nb in (2, 4, 8):
            q_off = [chunk_sum((c // nb) * nb, c) for c in range(_N_CHUNKS)]
            k_off = [chunk_sum(c + 1, (c // nb + 1) * nb) for c in range(_N_CHUNKS)]
            q_lv.append(scaled(q_dec, q_off))
            k_lv.append(scaled(k_dec, k_off))
        slab_decay = jnp.exp(chunk_sum(0, _N_CHUNKS))

        q_bf = [a.astype(BF16) for a in q_lv]
        k_bf = [a.astype(BF16) for a in k_lv]
        i_bf = i_ref[r0:r0 + SLAB, :]

        for h in range(HGRN_HEADS):
            cols = slice(h * HGRN_DIM, (h + 1) * HGRN_DIM)
            scores = None
            for lv in range(4):
                s_lv = lax.dot_general(q_bf[lv][:, cols], k_bf[lv][:, cols],
                                       (((1,), (1,)), ((), ())), preferred_element_type=F32)
                s_lv = jnp.where(masks[lv], s_lv, 0.0)
                scores = s_lv if scores is None else scores + s_lv
            ih = i_bf[:, cols]
            st = st_scr[h]
            o = jnp.dot(scores.astype(BF16), ih, preferred_element_type=F32)
            o = o + lax.dot_general(q_bf[4][:, cols], st.astype(BF16),
                                    (((1,), (1,)), ((), ())), preferred_element_type=F32)
            upd = lax.dot_general(ih, k_bf[4][:, cols],
                                  (((0,), (0,)), ((), ())), preferred_element_type=F32)
            st_scr[h] = st * slab_decay[:, cols] + upd
            ms = jnp.mean(o * o, axis=-1, keepdims=True)
            on = o * lax.rsqrt(ms + EPS) * ng_ref[:, cols]
            o_ref[r0:r0 + SLAB, cols] = (on * sg_ref[r0:r0 + SLAB, cols].astype(F32)).astype(BF16)


def _hgrn(zc, f, ng, l16, rows):
    T = f.shape[0]
    return pl.pallas_call(
        _hgrn_kernel,
        grid=(T // rows,),
        in_specs=[
            pl.BlockSpec((rows, COL), lambda i: (i, 1)),
            pl.BlockSpec((rows, COL), lambda i: (i, 0)),
            pl.BlockSpec((rows, COL), lambda i: (i, 2)),
            pl.BlockSpec((rows, COL), lambda i: (i, 3)),
            pl.BlockSpec((1, COL), lambda i: (0, 0)),
            pl.BlockSpec((SLAB, SLAB), lambda i: (0, 0)),
        ],
        out_specs=pl.BlockSpec((rows, COL), lambda i: (i, 0)),
        out_shape=jax.ShapeDtypeStruct((T, COL), BF16),
        scratch_shapes=[pltpu.VMEM((HGRN_HEADS, HGRN_DIM, HGRN_DIM), F32)],
        compiler_params=_cparams(("arbitrary",)),
        name="hgrn",
    )(zc, f, zc, zc, ng, l16)


_R_E1, _R_E2, _R_W1, _R_W2, _R_RANK1, _R_RANK2 = range(6)
_GROUP_LANE0 = N_EXPERTS


def _merge_kernel(ya_ref, yb_ref, ga0_ref, ga1_ref, gb0_ref, gb1_ref, x_ref,
                  wa_ref, wb_ref, wo_ref, g_ref, wr_ref, br_ref, tri_ref,
                  x1_ref, h2_ref, route_ref, cnt_ref, cnt_scr):
    step = pl.program_id(0)

    @pl.when(step == 0)
    def _():
        cnt_scr[...] = jnp.zeros_like(cnt_scr)

    pa = jnp.dot(ya_ref[...], wa_ref[...], preferred_element_type=F32)
    pb = jnp.dot(yb_ref[...], wb_ref[...], preferred_element_type=F32)
    ga = jnp.concatenate([ga0_ref[...], ga1_ref[...]], axis=1).astype(F32)
    gb = jnp.concatenate([gb0_ref[...], gb1_ref[...]], axis=1).astype(F32)
    merged = ga * pa + gb * pb
    x1 = x_ref[...] + jnp.dot(merged.astype(BF16), wo_ref[...], preferred_element_type=F32)
    x1_ref[...] = x1
    ms = jnp.mean(x1 * x1, axis=-1, keepdims=True)
    h2 = x1 * lax.rsqrt(ms + EPS) * g_ref[...]
    h2_ref[...] = h2
    logits = jnp.dot(h2.astype(BF16), wr_ref[...], preferred_element_type=F32) + br_ref[...]

    tm = logits.shape[0]
    lane = lax.broadcasted_iota(jnp.int32, (tm, LANES), 1).astype(F32)
    neg = jnp.float32(-jnp.inf)
    big = jnp.float32(1e9)
    is_group = (lane >= _GROUP_LANE0) & (lane < _GROUP_LANE0 + N_GROUPS)
    gl = jnp.where(is_group, logits, neg)
    gmax = jnp.max(gl, axis=-1, keepdims=True)
    g_sel = jnp.min(jnp.where(gl == gmax, lane, big), axis=-1, keepdims=True) - _GROUP_LANE0
    p_g = 1.0 / jnp.sum(jnp.exp(gl - gmax), axis=-1, keepdims=True)
    lo = g_sel * EXPERTS_PER_GROUP
    in_group = (lane >= lo) & (lane < lo + EXPERTS_PER_GROUP)
    el = jnp.where(in_group, logits, neg)
    m1 = jnp.max(el, axis=-1, keepdims=True)
    e1 = jnp.min(jnp.where(el == m1, lane, big), axis=-1, keepdims=True)
    el2 = jnp.where(lane == e1, neg, el)
    m2 = jnp.max(el2, axis=-1, keepdims=True)
    e2 = jnp.min(jnp.where(el2 == m2, lane, big), axis=-1, keepdims=True)
    ex = jnp.exp(m2 - m1)
    w1 = p_g / (1.0 + ex)
    w2 = p_g * ex / (1.0 + ex)

    oh1 = (lane == e1).astype(F32)
    oh2 = (lane == e2).astype(F32)
    both = oh1 + oh2
    before = jnp.dot(tri_ref[...], both.astype(BF16), preferred_element_type=F32) + cnt_scr[...]
    rank1 = jnp.sum(oh1 * before, axis=-1, keepdims=True)
    rank2 = jnp.sum(oh2 * before, axis=-1, keepdims=True)
    cnt = cnt_scr[...] + jnp.sum(both, axis=0, keepdims=True)
    cnt_scr[...] = cnt
    cnt_ref[...] = cnt

    rec = jnp.zeros((tm, LANES), F32)
    for idx, val in ((_R_E1, e1), (_R_E2, e2), (_R_W1, w1), (_R_W2, w2),
                     (_R_RANK1, rank1), (_R_RANK2, rank2)):
        rec = jnp.where(lane == idx, val, rec)
    route_ref[...] = rec


def _merge(zc, yb, x2, wa, wb, wo, g, wr, br, tri, tm):
    T = x2.shape[0]
    const = lambda shape: pl.BlockSpec(shape, lambda i: (0,) * len(shape),
                                       pipeline_mode=pl.Buffered(1))
    return pl.pallas_call(
        _merge_kernel,
        grid=(T // tm,),
        in_specs=[
            pl.BlockSpec((tm, COL), lambda i: (i, 0)),
            pl.BlockSpec((tm, COL), lambda i: (i, 0)),
            pl.BlockSpec((tm, COL), lambda i: (i, 4)),
            pl.BlockSpec((tm, COL), lambda i: (i, 5)),
            pl.BlockSpec((tm, COL), lambda i: (i, 6)),
            pl.BlockSpec((tm, COL), lambda i: (i, 7)),
            pl.BlockSpec((tm, D_MODEL), lambda i: (i, 0)),
            const((GMLP_WIDTH, D_MODEL)),
            const((HGRN_WIDTH, D_MODEL)),
            const((D_MODEL, D_MODEL)),
            const((1, D_MODEL)),
            const((D_MODEL, LANES)),
            const((1, LANES)),
            const((tm, tm)),
        ],
        out_specs=[
            pl.BlockSpec((tm, D_MODEL), lambda i: (i, 0)),
            pl.BlockSpec((tm, D_MODEL), lambda i: (i, 0)),
            pl.BlockSpec((tm, LANES), lambda i: (i, 0)),
            pl.BlockSpec((1, LANES), lambda i: (0, 0)),
        ],
        out_shape=[
            jax.ShapeDtypeStruct((T, D_MODEL), F32),
            jax.ShapeDtypeStruct((T, D_MODEL), F32),
            jax.ShapeDtypeStruct((T, LANES), F32),
            jax.ShapeDtypeStruct((1, LANES), F32),
        ],
        scratch_shapes=[pltpu.VMEM((1, LANES), F32)],
        compiler_params=_cparams(("arbitrary",)),
        name="merge_route",
    )(zc, yb, zc, zc, zc, zc, x2, wa, wb, wo, g, wr, br, tri)


_DMA_UNROLL = 8


def _dispatch_kernel(zstart_ref, zend_ref, nused_ref, dest_ref, h_ref, xb_ref, zeros_scr, sem, zsem):
    tb = h_ref.shape[0]
    total_blocks = xb_ref.shape[0] // MOE_BLOCK

    @pl.when(pl.program_id(0) == 0)
    def _():
        zeros_scr[...] = jnp.zeros_like(zeros_scr)

        def for_each(fn):
            def tail(j, c):
                row = pl.multiple_of(j * MOE_BLOCK, MOE_BLOCK)
                fn(pltpu.make_async_copy(zeros_scr, xb_ref.at[pl.ds(row, MOE_BLOCK)], zsem))
                return c

            def expert(e, c):
                zs = zstart_ref[e]
                up = lax.shift_left(lax.shift_right_logical(zs + 7, 3), 3)

                def one(r, c2):
                    fn(pltpu.make_async_copy(zeros_scr.at[pl.ds(0, 1)], xb_ref.at[pl.ds(r, 1)], zsem))
                    return c2
                lax.fori_loop(zs, up, one, 0)
                rem = zend_ref[e] - up
                off = up
                for size in (64, 32, 16, 8):
                    has = (rem & size) != 0

                    @pl.when(has)
                    def _(off=off, size=size):
                        fn(pltpu.make_async_copy(zeros_scr.at[pl.ds(0, size)],
                                                 xb_ref.at[pl.ds(pl.multiple_of(off, 8), size)], zsem))
                    off = off + jnp.where(has, size, 0)
                return c

            lax.fori_loop(nused_ref[0], total_blocks, tail, 0)
            lax.fori_loop(0, N_EXPERTS, expert, 0)

        for_each(lambda cp: cp.start())
        for_each(lambda cp: cp.wait())

    def start(t, c):
        for k in range(2):
            d = dest_ref[0, 0, 2 * t + k]
            pltpu.make_async_copy(h_ref.at[pl.ds(t, 1)], xb_ref.at[pl.ds(d, 1)], sem).start(priority=k)
        return c

    def wait(t, c):
        for k in range(2):
            pltpu.make_async_copy(h_ref.at[pl.ds(0, 1)], xb_ref.at[pl.ds(0, 1)], sem).wait()
        return c

    lax.fori_loop(0, tb, start, 0, unroll=_DMA_UNROLL)
    lax.fori_loop(0, tb, wait, 0, unroll=_DMA_UNROLL)


def _dispatch(zstart, zend, n_used, dest3, h2, n_blocks, tb):
    T = h2.shape[0]
    grid_spec = pltpu.PrefetchScalarGridSpec(
        num_scalar_prefetch=3,
        grid=(T // tb,),
        in_specs=[
            pl.BlockSpec((1, 1, 2 * tb), lambda i, zs, ze, nu: (i, 0, 0), memory_space=pltpu.SMEM),
            pl.BlockSpec((tb, D_MODEL), lambda i, zs, ze, nu: (i, 0)),
        ],
        out_specs=pl.BlockSpec(memory_space=pl.ANY),
        scratch_shapes=[pltpu.VMEM((MOE_BLOCK, D_MODEL), F32),
                        pltpu.SemaphoreType.DMA(()), pltpu.SemaphoreType.DMA(())],
    )
    return pl.pallas_call(
        _dispatch_kernel,
        grid_spec=grid_spec,
        out_shape=jax.ShapeDtypeStruct((n_blocks * MOE_BLOCK, D_MODEL), F32),
        compiler_params=_cparams(("arbitrary",)),
        name="dispatch",
    )(zstart, zend, n_used, dest3, h2)


def _moe_kernel(be_ref, first_ref, nexte_ref, nused_ref, x_ref, wg_hbm, wu_hbm, wd_hbm, y_ref,
                stg_g, stg_u, stg_d, wg_bf, wu_bf, wd_bf, sems):
    b = pl.program_id(0)
    used = b < nused_ref[0]

    streams = ((wg_hbm, stg_g, wg_bf), (wu_hbm, stg_u, wu_bf), (wd_hbm, stg_d, wd_bf))

    def weight_copy(i, e):
        hbm, stg, _ = streams[i]
        return pltpu.make_async_copy(hbm.at[e], stg, sems.at[i])

    @pl.when(b == 0)
    def _():
        for i in range(3):
            weight_copy(i, be_ref[0]).start(priority=i % 2)

    @pl.when(used & (first_ref[b] == 1))
    def _():
        nxt = nexte_ref[b]
        for i in range(3):
            weight_copy(i, be_ref[b]).wait()
            streams[i][2][...] = streams[i][1][...].astype(BF16)

            @pl.when(nxt >= 0)
            def _():
                weight_copy(i, nxt).start(priority=i % 2)

    @pl.when(used)
    def _():
        x = x_ref[...].astype(BF16)
        a = jnp.dot(x, wg_bf[...], preferred_element_type=F32)
        g = jnp.dot(x, wu_bf[...], preferred_element_type=F32)
        hdn = (jax.nn.silu(a) * g).astype(BF16)
        y_ref[...] = jnp.dot(hdn, wd_bf[...], preferred_element_type=F32)

    @pl.when(jnp.logical_not(used))
    def _():
        y_ref[...] = jnp.zeros_like(y_ref)


def _moe(block_e, first, next_e, n_used, xb, wg, wu, wd):
    n_blocks = block_e.shape[0]
    P = n_blocks * MOE_BLOCK
    grid_spec = pltpu.PrefetchScalarGridSpec(
        num_scalar_prefetch=4,
        grid=(n_blocks,),
        in_specs=[
            pl.BlockSpec((MOE_BLOCK, D_MODEL), lambda b, be, fi, ne, nu: (jnp.minimum(b, nu[0] - 1), 0)),
            pl.BlockSpec(memory_space=pl.ANY),
            pl.BlockSpec(memory_space=pl.ANY),
            pl.BlockSpec(memory_space=pl.ANY),
        ],
        out_specs=pl.BlockSpec((MOE_BLOCK, D_MODEL), lambda b, be, fi, ne, nu: (b, 0)),
        scratch_shapes=[
            pltpu.VMEM((D_MODEL, D_EXPERT), F32),
            pltpu.VMEM((D_MODEL, D_EXPERT), F32),
            pltpu.VMEM((D_EXPERT, D_MODEL), F32),
            pltpu.VMEM((D_MODEL, D_EXPERT), BF16),
            pltpu.VMEM((D_MODEL, D_EXPERT), BF16),
            pltpu.VMEM((D_EXPERT, D_MODEL), BF16),
            pltpu.SemaphoreType.DMA((3,)),
        ],
    )
    return pl.pallas_call(
        _moe_kernel,
        grid_spec=grid_spec,
        out_shape=jax.ShapeDtypeStruct((P, D_MODEL), F32),
        compiler_params=_cparams(("arbitrary",)),
        name="moe_experts",
    )(block_e, first, next_e, n_used, xb, wg, wu, wd)


def _combine_kernel(dest_ref, dnext_ref, yb_ref, x1_ref, route_ref, g_ref, out_ref, buf, sems):
    tb = x1_ref.shape[0]
    step = pl.program_id(0)
    slot = step % 2

    def issue(d_ref, s):
        def body(t, c):
            for k in range(2):
                d = d_ref[0, 0, 2 * t + k]
                pltpu.make_async_copy(yb_ref.at[pl.ds(d, 1)], buf.at[s, k, pl.ds(t, 1)],
                                      sems.at[s]).start(priority=k)
            return c
        lax.fori_loop(0, tb, body, 0, unroll=_DMA_UNROLL)

    @pl.when(step == 0)
    def _():
        issue(dest_ref, 0)

    @pl.when(step + 1 < pl.num_programs(0))
    def _():
        issue(dnext_ref, 1 - slot)

    def drain(t, c):
        for k in range(2):
            pltpu.make_async_copy(yb_ref.at[pl.ds(0, 1)], buf.at[slot, k, pl.ds(0, 1)], sems.at[slot]).wait()
        return c

    lax.fori_loop(0, tb, drain, 0, unroll=_DMA_UNROLL)

    rec = route_ref[...]
    w1 = rec[:, _R_W1:_R_W1 + 1]
    w2 = rec[:, _R_W2:_R_W2 + 1]
    x2 = x1_ref[...] + (buf[slot, 0] * w1 + buf[slot, 1] * w2)
    ms = jnp.mean(x2 * x2, axis=-1, keepdims=True)
    out_ref[...] = x2 * lax.rsqrt(ms + EPS) * g_ref[...]


def _combine(dest3, yb, x1, route, g, tb):
    T = x1.shape[0]
    n_steps = T // tb
    return pl.pallas_call(
        _combine_kernel,
        grid=(n_steps,),
        in_specs=[
            pl.BlockSpec((1, 1, 2 * tb), lambda i: (i, 0, 0), memory_space=pltpu.SMEM),
            pl.BlockSpec((1, 1, 2 * tb), lambda i: (jnp.minimum(i + 1, n_steps - 1), 0, 0),
                         memory_space=pltpu.SMEM),
            pl.BlockSpec(memory_space=pl.ANY),
            pl.BlockSpec((tb, D_MODEL), lambda i: (i, 0)),
            pl.BlockSpec((tb, LANES), lambda i: (i, 0)),
            pl.BlockSpec((1, D_MODEL), lambda i: (0, 0)),
        ],
        out_specs=pl.BlockSpec((tb, D_MODEL), lambda i: (i, 0)),
        out_shape=jax.ShapeDtypeStruct((T, D_MODEL), F32),
        scratch_shapes=[pltpu.VMEM((2, 2, tb, D_MODEL), F32), pltpu.SemaphoreType.DMA((2,))],
        compiler_params=_cparams(("arbitrary",)),
        name="combine",
    )(dest3, dest3, yb, x1, route, g)


def _tile(T, pref):
    return pref if T % pref == 0 else T


def kernel(x, norm_mix_g, w_in, gmlp_ln_g, gmlp_ln_b, w_spatial, b_spatial, hgrn_lb_logits,
           hgrn_norm_g, w_branch_a, w_branch_b, w_out, norm_ffn_g, w_router_group, b_router_group,
           w_router_expert, b_router_expert, w_expert_gate, w_expert_up, w_expert_down, norm_final_g):
    bn, s, d = x.shape
    T = bn * s
    x2 = x.reshape(T, d)
    l = 0

    lb = jnp.cumsum(jax.nn.softmax(hgrn_lb_logits.astype(F32), axis=0), axis=0)[l][None, :]
    causal = jnp.tril(jnp.ones((GMLP_CHUNK, GMLP_CHUNK), dtype=bool))
    ws_bf = jnp.where(causal[None], w_spatial[l], 0.0).astype(BF16)
    bs_full = jnp.repeat(b_spatial[l].T, GMLP_WIDTH // GMLP_GROUPS, axis=1)
    r = np.arange(SLAB)
    l16 = jnp.asarray(((r[:, None] // HGRN_CHUNK == r[None, :] // HGRN_CHUNK)
                       & (r[None, :] <= r[:, None])).astype(np.float32), dtype=BF16)
    wr = jnp.concatenate([w_router_expert[l], w_router_group[l],
                          jnp.zeros((d, LANES - N_EXPERTS - N_GROUPS), F32)], axis=1).astype(BF16)
    br = jnp.concatenate([b_router_expert[l], b_router_group[l],
                          jnp.zeros((LANES - N_EXPERTS - N_GROUPS,), F32)])[None, :].astype(F32)

    tm1 = _tile(T, 512)
    zc, f = _inproj(x2, norm_mix_g[l][None, :], w_in[l].astype(BF16), gmlp_ln_g[l][None, :],
                    gmlp_ln_b[l][None, :], ws_bf, bs_full, lb, tm1)

    y_b = _hgrn(zc, f, hgrn_norm_g[l][None, :], l16, _tile(T, 256))

    tm3 = _tile(T, 256)
    rr = np.arange(tm3)
    tri = jnp.asarray((rr[None, :] < rr[:, None]).astype(np.float32), dtype=BF16)
    x1, h2, route, counts = _merge(
        zc, y_b, x2, w_branch_a[l].astype(BF16), w_branch_b[l].astype(BF16), w_out[l].astype(BF16),
        norm_ffn_g[l][None, :], wr, br, tri, tm3)

    counts = counts[0, :N_EXPERTS].astype(jnp.int32)
    padded = ((counts + MOE_BLOCK - 1) // MOE_BLOCK) * MOE_BLOCK
    pad_end = jnp.cumsum(padded)
    pad_start = pad_end - padded
    n_assign = 2 * T
    n_blocks = (n_assign + N_EXPERTS * (MOE_BLOCK - 1) + MOE_BLOCK - 1) // MOE_BLOCK
    block_start = jnp.arange(n_blocks, dtype=jnp.int32) * MOE_BLOCK
    block_e = jnp.minimum(jnp.sum(pad_end[None, :] <= block_start[:, None], axis=1),
                          N_EXPERTS - 1).astype(jnp.int32)
    n_used = (pad_end[-1:] // MOE_BLOCK).astype(jnp.int32)
    first = jnp.concatenate([jnp.ones((1,), jnp.int32),
                             (block_e[1:] != block_e[:-1]).astype(jnp.int32)])
    e_ids = jnp.arange(N_EXPERTS, dtype=jnp.int32)
    later = lax.cummin(jnp.where(padded > 0, e_ids, N_EXPERTS), axis=0, reverse=True)
    next_of = jnp.concatenate([later[1:], jnp.full((1,), N_EXPERTS, jnp.int32)])
    next_of = jnp.where(next_of >= N_EXPERTS, -1, next_of).astype(jnp.int32)
    next_e = next_of[block_e]
    eid = route[:, _R_E1:_R_E2 + 1].astype(jnp.int32)
    rank = route[:, _R_RANK1:_R_RANK2 + 1].astype(jnp.int32)
    dest = jnp.sum(jnp.where(eid[:, :, None] == e_ids, pad_start, 0), axis=-1) + rank
    zstart = pad_start + counts

    tb_d = _tile(T, 1024)
    xb = _dispatch(zstart, pad_end, n_used, dest.reshape(T // tb_d, 1, 2 * tb_d), h2, n_blocks, tb_d)
    yb = _moe(block_e, first, next_e, n_used, xb, w_expert_gate[l], w_expert_up[l], w_expert_down[l])
    tb_c = _tile(T, 512)
    out = _combine(dest.reshape(T // tb_c, 1, 2 * tb_c), yb, x1, route, norm_final_g[None, :], tb_c)
    return out.reshape(bn, s, d)
```

```python
import functools

import jax
import jax.numpy as jnp
import numpy as np
from jax import lax
from jax.experimental import pallas as pl
from jax.experimental.pallas import tpu as pltpu

F32 = jnp.float32
BF16 = jnp.bfloat16

D_MODEL = 2048
EPS = 1e-6
GMLP_WIDTH = 1024
GMLP_GROUPS = 8
GMLP_CHUNK = 128
HGRN_HEADS = 8
HGRN_DIM = 128
HGRN_WIDTH = HGRN_HEADS * HGRN_DIM
HGRN_CHUNK = 16
SLAB = 128
N_GROUPS = 8
EXPERTS_PER_GROUP = 8
N_EXPERTS = N_GROUPS * EXPERTS_PER_GROUP
D_EXPERT = 1024
MOE_BLOCK = 128
COL = 1024
N_COL_STEPS = 10
LANES = 128

VMEM_LIMIT = 56 * 1024 * 1024


def _cparams(sem):
    return pltpu.CompilerParams(dimension_semantics=sem, vmem_limit_bytes=VMEM_LIMIT)


_ROW_CHUNK = 256


def _inproj_kernel(x_ref, g_ref, w_ref, lng_ref, lnb_ref, ws_ref, bs_ref, lb_ref,
                   zc_ref, f_ref, h_scr, u_scr, v_scr):
    j = pl.program_id(1)
    tm = x_ref.shape[0]
    chunks = [slice(r, r + _ROW_CHUNK) for r in range(0, tm, _ROW_CHUNK)]

    def proj(rows):
        return jnp.dot(h_scr[rows, :], w_ref[...], preferred_element_type=F32)

    @pl.when(j == 0)
    def _():
        x = x_ref[...]
        ms = jnp.mean(x * x, axis=-1, keepdims=True)
        h_scr[...] = (x * lax.rsqrt(ms + EPS) * g_ref[...]).astype(BF16)
        for rows in chunks:
            u_scr[rows, :] = jax.nn.gelu(proj(rows))

    @pl.when(j == 1)
    def _():
        for rows in chunks:
            gv = jax.nn.gelu(proj(rows))
            mu = jnp.mean(gv, axis=-1, keepdims=True)
            var = jnp.mean(jnp.square(gv - mu), axis=-1, keepdims=True)
            vn = (gv - mu) * lax.rsqrt(var + EPS) * lng_ref[...] + lnb_ref[...]
            v_scr[rows, :] = vn.astype(BF16)
            for c in range(rows.start, rows.stop, GMLP_CHUNK):
                crows = slice(c, c + GMLP_CHUNK)
                for g in range(GMLP_GROUPS):
                    cols = slice(g * LANES, (g + 1) * LANES)
                    sv = jnp.dot(ws_ref[g], v_scr[crows, cols], preferred_element_type=F32)
                    sv = sv + bs_ref[:, cols]
                    zc_ref[crows, cols] = (u_scr[crows, cols] * sv).astype(BF16)

    @pl.when((j == 2) | (j == 4))
    def _():
        for rows in chunks:
            zc_ref[rows, :] = proj(rows).astype(BF16)

    @pl.when(j == 3)
    def _():
        lb = lb_ref[...]
        for rows in chunks:
            f_ref[rows, :] = lb + (1.0 - lb) * jax.nn.sigmoid(proj(rows))

    @pl.when(j == 5)
    def _():
        for rows in chunks:
            zc_ref[rows, :] = jax.nn.silu(proj(rows)).astype(BF16)

    @pl.when(j >= 6)
    def _():
        for rows in chunks:
            zc_ref[rows, :] = jax.nn.sigmoid(proj(rows)).astype(BF16)


def _zc_col(j):
    return jnp.where(j < 2, 0, jnp.where(j < 4, 1, j - 2))


def _inproj(x2, g, w_bf, lng, lnb, ws_bf, bs_full, lb, tm):
    T = x2.shape[0]
    grid = (T // tm, N_COL_STEPS)
    return pl.pallas_call(
        _inproj_kernel,
        grid=grid,
        in_specs=[
            pl.BlockSpec((tm, D_MODEL), lambda i, j: (i, 0)),
            pl.BlockSpec((1, D_MODEL), lambda i, j: (0, 0)),
            pl.BlockSpec((D_MODEL, COL), lambda i, j: (0, j)),
            pl.BlockSpec((1, COL), lambda i, j: (0, 0)),
            pl.BlockSpec((1, COL), lambda i, j: (0, 0)),
            pl.BlockSpec((GMLP_GROUPS, GMLP_CHUNK, GMLP_CHUNK), lambda i, j: (0, 0, 0)),
            pl.BlockSpec((GMLP_CHUNK, COL), lambda i, j: (0, 0)),
            pl.BlockSpec((1, COL), lambda i, j: (0, 0)),
        ],
        out_specs=[
            pl.BlockSpec((tm, COL), lambda i, j: (i, _zc_col(j))),
            pl.BlockSpec((tm, COL), lambda i, j: (i, 0)),
        ],
        out_shape=[
            jax.ShapeDtypeStruct((T, 8 * COL), BF16),
            jax.ShapeDtypeStruct((T, COL), F32),
        ],
        scratch_shapes=[
            pltpu.VMEM((tm, D_MODEL), BF16),
            pltpu.VMEM((tm, COL), F32),
            pltpu.VMEM((tm, COL), BF16),
        ],
        compiler_params=_cparams(("arbitrary", "arbitrary")),
        name="inproj",
    )(x2, g, w_bf, lng, lnb, ws_bf, bs_full, lb)


_N_CHUNKS = SLAB // HGRN_CHUNK


def _hgrn_kernel(q_ref, f_ref, i_ref, sg_ref, ng_ref, l16_ref, o_ref, st_scr):
    step = pl.program_id(0)

    @pl.when(step == 0)
    def _():
        st_scr[...] = jnp.zeros_like(st_scr)

    rows_per_step = q_ref.shape[0]
    row = lax.broadcasted_iota(jnp.int32, (SLAB, SLAB), 0)
    col = lax.broadcasted_iota(jnp.int32, (SLAB, SLAB), 1)
    rc = row // HGRN_CHUNK
    cc = col // HGRN_CHUNK
    masks = [(rc == cc) & (col <= row)]
    for nb in (1, 2, 4):
        rs = rc // nb
        cs = cc // nb
        masks.append(((rs % 2) == 1) & (cs == rs - 1))

    for sl in range(rows_per_step // SLAB):
        r0 = sl * SLAB
        f = f_ref[r0:r0 + SLAB, :]
        lf = jnp.log(f)
        kk = 1.0 - f
        lf_hi = lf.astype(BF16)
        lf_lo = (lf - lf_hi.astype(F32)).astype(BF16)
        l16 = l16_ref[...]
        e16 = (jnp.dot(l16, lf_hi, preferred_element_type=F32)
               + jnp.dot(l16, lf_lo, preferred_element_type=F32))
        tot = [e16[c * HGRN_CHUNK + HGRN_CHUNK - 1:c * HGRN_CHUNK + HGRN_CHUNK, :]
               for c in range(_N_CHUNKS)]
        r16 = jnp.concatenate(
            [tot[c] - e16[c * HGRN_CHUNK:(c + 1) * HGRN_CHUNK, :] for c in range(_N_CHUNKS)], axis=0)
        qf = q_ref[r0:r0 + SLAB, :].astype(F32)
        q_dec = qf * jnp.exp(e16)
        k_inv = kk * jnp.exp(-e16)
        k_dec = kk * jnp.exp(r16)

        def chunk_sum(lo, hi):
            acc = None
            for c in range(lo, hi):
                acc = tot[c] if acc is None else acc + tot[c]
            return acc

        def scaled(base, offsets):
            parts = []
            for c in range(_N_CHUNKS):
                blk = base[c * HGRN_CHUNK:(c + 1) * HGRN_CHUNK, :]
                if offsets[c] is not None:
                    blk = blk * jnp.exp(offsets[c])
                parts.append(blk)
            return jnp.concatenate(parts, axis=0)

        q_lv = [q_dec, q_dec]
        k_lv = [k_inv, k_dec]
        for nb in (2, 4, 8):
            q_off = [chunk_sum((c // nb) * nb, c) for c in range(_N_CHUNKS)]
            k_off = [chunk_sum(c + 1, (c // nb + 1) * nb) for c in range(_N_CHUNKS)]
            q_lv.append(scaled(q_dec, q_off))
            k_lv.append(scaled(k_dec, k_off))
        slab_decay = jnp.exp(chunk_sum(0, _N_CHUNKS))

        q_bf = [a.astype(BF16) for a in q_lv]
        k_bf = [a.astype(BF16) for a in k_lv]
        i_bf = i_ref[r0:r0 + SLAB, :]

        for h in range(HGRN_HEADS):
            cols = slice(h * HGRN_DIM, (h + 1) * HGRN_DIM)
            scores = None
            for lv in range(4):
                s_lv = lax.dot_general(q_bf[lv][:, cols], k_bf[lv][:, cols],
                                       (((1,), (1,)), ((), ())), preferred_element_type=F32)
                s_lv = jnp.where(masks[lv], s_lv, 0.0)
                scores = s_lv if scores is None else scores + s_lv
            ih = i_bf[:, cols]
            st = st_scr[h]
            o = jnp.dot(scores.astype(BF16), ih, preferred_element_type=F32)
            o = o + lax.dot_general(q_bf[4][:, cols], st.astype(BF16),
                                    (((1,), (1,)), ((), ())), preferred_element_type=F32)
            upd = lax.dot_general(ih, k_bf[4][:, cols],
                                  (((0,), (0,)), ((), ())), preferred_element_type=F32)
            st_scr[h] = st * slab_decay[:, cols] + upd
            ms = jnp.mean(o * o, axis=-1, keepdims=True)
            on = o * lax.rsqrt(ms + EPS) * ng_ref[:, cols]
            o_ref[r0:r0 + SLAB, cols] = (on * sg_ref[r0:r0 + SLAB, cols].astype(F32)).astype(BF16)


def _hgrn(zc, f, ng, l16, rows):
    T = f.shape[0]
    return pl.pallas_call(
        _hgrn_kernel,
        grid=(T // rows,),
        in_specs=[
            pl.BlockSpec((rows, COL), lambda i: (i, 1)),
            pl.BlockSpec((rows, COL), lambda i: (i, 0)),
            pl.BlockSpec((rows, COL), lambda i: (i, 2)),
            pl.BlockSpec((rows, COL), lambda i: (i, 3)),
            pl.BlockSpec((1, COL), lambda i: (0, 0)),
            pl.BlockSpec((SLAB, SLAB), lambda i: (0, 0)),
        ],
        out_specs=pl.BlockSpec((rows, COL), lambda i: (i, 0)),
        out_shape=jax.ShapeDtypeStruct((T, COL), BF16),
        scratch_shapes=[pltpu.VMEM((HGRN_HEADS, HGRN_DIM, HGRN_DIM), F32)],
        compiler_params=_cparams(("arbitrary",)),
        name="hgrn",
    )(zc, f, zc, zc, ng, l16)


_R_E1, _R_E2, _R_W1, _R_W2, _R_RANK1, _R_RANK2 = range(6)
_GROUP_LANE0 = N_EXPERTS


def _merge_kernel(ya_ref, yb_ref, ga0_ref, ga1_ref, gb0_ref, gb1_ref, x_ref,
                  wa_ref, wb_ref, wo_ref, g_ref, wr_ref, br_ref, tri_ref,
                  x1_ref, h2_ref, route_ref, cnt_ref, cnt_scr):
    step = pl.program_id(0)

    @pl.when(step == 0)
    def _():
        cnt_scr[...] = jnp.zeros_like(cnt_scr)

    pa = jnp.dot(ya_ref[...], wa_ref[...], preferred_element_type=F32)
    pb = jnp.dot(yb_ref[...], wb_ref[...], preferred_element_type=F32)
    ga = jnp.concatenate([ga0_ref[...], ga1_ref[...]], axis=1).astype(F32)
    gb = jnp.concatenate([gb0_ref[...], gb1_ref[...]], axis=1).astype(F32)
    merged = ga * pa + gb * pb
    x1 = x_ref[...] + jnp.dot(merged.astype(BF16), wo_ref[...], preferred_element_type=F32)
    x1_ref[...] = x1
    ms = jnp.mean(x1 * x1, axis=-1, keepdims=True)
    h2 = x1 * lax.rsqrt(ms + EPS) * g_ref[...]
    h2_ref[...] = h2
    logits = jnp.dot(h2.astype(BF16), wr_ref[...], preferred_element_type=F32) + br_ref[...]

    tm = logits.shape[0]
    lane = lax.broadcasted_iota(jnp.int32, (tm, LANES), 1).astype(F32)
    neg = jnp.float32(-jnp.inf)
    big = jnp.float32(1e9)
    is_group = (lane >= _GROUP_LANE0) & (lane < _GROUP_LANE0 + N_GROUPS)
    gl = jnp.where(is_group, logits, neg)
    gmax = jnp.max(gl, axis=-1, keepdims=True)
    g_sel = jnp.min(jnp.where(gl == gmax, lane, big), axis=-1, keepdims=True) - _GROUP_LANE0
    p_g = 1.0 / jnp.sum(jnp.exp(gl - gmax), axis=-1, keepdims=True)
    lo = g_sel * EXPERTS_PER_GROUP
    in_group = (lane >= lo) & (lane < lo + EXPERTS_PER_GROUP)
    el = jnp.where(in_group, logits, neg)
    m1 = jnp.max(el, axis=-1, keepdims=True)
    e1 = jnp.min(jnp.where(el == m1, lane, big), axis=-1, keepdims=True)
    el2 = jnp.where(lane == e1, neg, el)
    m2 = jnp.max(el2, axis=-1, keepdims=True)
    e2 = jnp.min(jnp.where(el2 == m2, lane, big), axis=-1, keepdims=True)
    ex = jnp.exp(m2 - m1)
    w1 = p_g / (1.0 + ex)
    w2 = p_g * ex / (1.0 + ex)

    oh1 = (lane == e1).astype(F32)
    oh2 = (lane == e2).astype(F32)
    both = oh1 + oh2
    before = jnp.dot(tri_ref[...], both.astype(BF16), preferred_element_type=F32) + cnt_scr[...]
    rank1 = jnp.sum(oh1 * before, axis=-1, keepdims=True)
    rank2 = jnp.sum(oh2 * before, axis=-1, keepdims=True)
    cnt = cnt_scr[...] + jnp.sum(both, axis=0, keepdims=True)
    cnt_scr[...] = cnt
    cnt_ref[...] = cnt

    rec = jnp.zeros((tm, LANES), F32)
    for idx, val in ((_R_E1, e1), (_R_E2, e2), (_R_W1, w1), (_R_W2, w2),
                     (_R_RANK1, rank1), (_R_RANK2, rank2)):
        rec = jnp.where(lane == idx, val, rec)
    route_ref[...] = rec


def _merge(zc, yb, x2, wa, wb, wo, g, wr, br, tri, tm):
    T = x2.shape[0]
    const = lambda shape: pl.BlockSpec(shape, lambda i: (0,) * len(shape),
                                       pipeline_mode=pl.Buffered(1))
    return pl.pallas_call(
        _merge_kernel,
        grid=(T // tm,),
        in_specs=[
            pl.BlockSpec((tm, COL), lambda i: (i, 0)),
            pl.BlockSpec((tm, COL), lambda i: (i, 0)),
            pl.BlockSpec((tm, COL), lambda i: (i, 4)),
            pl.BlockSpec((tm, COL), lambda i: (i, 5)),
            pl.BlockSpec((tm, COL), lambda i: (i, 6)),
            pl.BlockSpec((tm, COL), lambda i: (i, 7)),
            pl.BlockSpec((tm, D_MODEL), lambda i: (i, 0)),
            const((GMLP_WIDTH, D_MODEL)),
            const((HGRN_WIDTH, D_MODEL)),
            const((D_MODEL, D_MODEL)),
            const((1, D_MODEL)),
            const((D_MODEL, LANES)),
            const((1, LANES)),
            const((tm, tm)),
        ],
        out_specs=[
            pl.BlockSpec((tm, D_MODEL), lambda i: (i, 0)),
            pl.BlockSpec((tm, D_MODEL), lambda i: (i, 0)),
            pl.BlockSpec((tm, LANES), lambda i: (i, 0)),
            pl.BlockSpec((1, LANES), lambda i: (0, 0)),
        ],
        out_shape=[
            jax.ShapeDtypeStruct((T, D_MODEL), F32),
            jax.ShapeDtypeStruct((T, D_MODEL), F32),
            jax.ShapeDtypeStruct((T, LANES), F32),
            jax.ShapeDtypeStruct((1, LANES), F32),
        ],
        scratch_shapes=[pltpu.VMEM((1, LANES), F32)],
        compiler_params=_cparams(("arbitrary",)),
        name="merge_route",
    )(zc, yb, zc, zc, zc, zc, x2, wa, wb, wo, g, wr, br, tri)


_DMA_UNROLL = 8


def _dispatch_kernel(zstart_ref, zend_ref, nused_ref, dest_ref, h_ref, xb_ref, zeros_scr, sem, zsem):
    tb = h_ref.shape[0]
    total_blocks = xb_ref.shape[0] // MOE_BLOCK

    @pl.when(pl.program_id(0) == 0)
    def _():
        zeros_scr[...] = jnp.zeros_like(zeros_scr)

        def for_each(fn):
            def tail(j, c):
                row = pl.multiple_of(j * MOE_BLOCK, MOE_BLOCK)
                fn(pltpu.make_async_copy(zeros_scr, xb_ref.at[pl.ds(row, MOE_BLOCK)], zsem))
                return c

            def expert(e, c):
                zs = zstart_ref[e]
                up = lax.shift_left(lax.shift_right_logical(zs + 7, 3), 3)

                def one(r, c2):
                    fn(pltpu.make_async_copy(zeros_scr.at[pl.ds(0, 1)], xb_ref.at[pl.ds(r, 1)], zsem))
                    return c2
                lax.fori_loop(zs, up, one, 0)
                rem = zend_ref[e] - up
                off = up
                for size in (64, 32, 16, 8):
                    has = (rem & size) != 0

                    @pl.when(has)
                    def _(off=off, size=size):
                        fn(pltpu.make_async_copy(zeros_scr.at[pl.ds(0, size)],
                                                 xb_ref.at[pl.ds(pl.multiple_of(off, 8), size)], zsem))
                    off = off + jnp.where(has, size, 0)
                return c

            lax.fori_loop(nused_ref[0], total_blocks, tail, 0)
            lax.fori_loop(0, N_EXPERTS, expert, 0)

        for_each(lambda cp: cp.start())
        for_each(lambda cp: cp.wait())

    def start(t, c):
        for k in range(2):
            d = dest_ref[0, 0, 2 * t + k]
            pltpu.make_async_copy(h_ref.at[pl.ds(t, 1)], xb_ref.at[pl.ds(d, 1)], sem).start(priority=k)
        return c

    def wait(t, c):
        for k in range(2):
            pltpu.make_async_copy(h_ref.at[pl.ds(0, 1)], xb_ref.at[pl.ds(0, 1)], sem).wait()
        return c

    lax.fori_loop(0, tb, start, 0, unroll=_DMA_UNROLL)
    lax.fori_loop(0, tb, wait, 0, unroll=_DMA_UNROLL)


def _dispatch(zstart, zend, n_used, dest3, h2, n_blocks, tb):
    T = h2.shape[0]
    grid_spec = pltpu.PrefetchScalarGridSpec(
        num_scalar_prefetch=3,
        grid=(T // tb,),
        in_specs=[
            pl.BlockSpec((1, 1, 2 * tb), lambda i, zs, ze, nu: (i, 0, 0), memory_space=pltpu.SMEM),
            pl.BlockSpec((tb, D_MODEL), lambda i, zs, ze, nu: (i, 0)),
        ],
        out_specs=pl.BlockSpec(memory_space=pl.ANY),
        scratch_shapes=[pltpu.VMEM((MOE_BLOCK, D_MODEL), F32),
                        pltpu.SemaphoreType.DMA(()), pltpu.SemaphoreType.DMA(())],
    )
    return pl.pallas_call(
        _dispatch_kernel,
        grid_spec=grid_spec,
        out_shape=jax.ShapeDtypeStruct((n_blocks * MOE_BLOCK, D_MODEL), F32),
        compiler_params=_cparams(("arbitrary",)),
        name="dispatch",
    )(zstart, zend, n_used, dest3, h2)


def _moe_kernel(be_ref, first_ref, nexte_ref, nused_ref, x_ref, wg_hbm, wu_hbm, wd_hbm, y_ref,
                stg_g, stg_u, stg_d, wg_bf, wu_bf, wd_bf, sems):
    b = pl.program_id(0)
    used = b < nused_ref[0]

    streams = ((wg_hbm, stg_g, wg_bf), (wu_hbm, stg_u, wu_bf), (wd_hbm, stg_d, wd_bf))

    def weight_copy(i, e):
        hbm, stg, _ = streams[i]
        return pltpu.make_async_copy(hbm.at[e], stg, sems.at[i])

    @pl.when(b == 0)
    def _():
        for i in range(3):
            weight_copy(i, be_ref[0]).start(priority=i % 2)

    @pl.when(used & (first_ref[b] == 1))
    def _():
        nxt = nexte_ref[b]
        for i in range(3):
            weight_copy(i, be_ref[b]).wait()
            streams[i][2][...] = streams[i][1][...].astype(BF16)

            @pl.when(nxt >= 0)
            def _():
                weight_copy(i, nxt).start(priority=i % 2)

    @pl.when(used)
    def _():
        x = x_ref[...].astype(BF16)
        a = jnp.dot(x, wg_bf[...], preferred_element_type=F32)
        g = jnp.dot(x, wu_bf[...], preferred_element_type=F32)
        hdn = (jax.nn.silu(a) * g).astype(BF16)
        y_ref[...] = jnp.dot(hdn, wd_bf[...], preferred_element_type=F32)

    @pl.when(jnp.logical_not(used))
    def _():
        y_ref[...] = jnp.zeros_like(y_ref)


def _moe(block_e, first, next_e, n_used, xb, wg, wu, wd):
    n_blocks = block_e.shape[0]
    P = n_blocks * MOE_BLOCK
    grid_spec = pltpu.PrefetchScalarGridSpec(
        num_scalar_prefetch=4,
        grid=(n_blocks,),
        in_specs=[
            pl.BlockSpec((MOE_BLOCK, D_MODEL), lambda b, be, fi, ne, nu: (jnp.minimum(b, nu[0] - 1), 0)),
            pl.BlockSpec(memory_space=pl.ANY),
            pl.BlockSpec(memory_space=pl.ANY),
            pl.BlockSpec(memory_space=pl.ANY),
        ],
        out_specs=pl.BlockSpec((MOE_BLOCK, D_MODEL), lambda b, be, fi, ne, nu: (b, 0)),
        scratch_shapes=[
            pltpu.VMEM((D_MODEL, D_EXPERT), F32),
            pltpu.VMEM((D_MODEL, D_EXPERT), F32),
            pltpu.VMEM((D_EXPERT, D_MODEL), F32),
            pltpu.VMEM((D_MODEL, D_EXPERT), BF16),
            pltpu.VMEM((D_MODEL, D_EXPERT), BF16),
            pltpu.VMEM((D_EXPERT, D_MODEL), BF16),
            pltpu.SemaphoreType.DMA((3,)),
        ],
    )
    return pl.pallas_call(
        _moe_kernel,
        grid_spec=grid_spec,
        out_shape=jax.ShapeDtypeStruct((P, D_MODEL), F32),
        compiler_params=_cparams(("arbitrary",)),
        name="moe_experts",
    )(block_e, first, next_e, n_used, xb, wg, wu, wd)


def _combine_kernel(dest_ref, dnext_ref, yb_ref, x1_ref, route_ref, g_ref, out_ref, buf, sems):
    tb = x1_ref.shape[0]
    step = pl.program_id(0)
    slot = step % 2

    def issue(d_ref, s):
        def body(t, c):
            for k in range(2):
                d = d_ref[0, 0, 2 * t + k]
                pltpu.make_async_copy(yb_ref.at[pl.ds(d, 1)], buf.at[s, k, pl.ds(t, 1)],
                                      sems.at[s]).start(priority=k)
            return c
        lax.fori_loop(0, tb, body, 0, unroll=_DMA_UNROLL)

    @pl.when(step == 0)
    def _():
        issue(dest_ref, 0)

    @pl.when(step + 1 < pl.num_programs(0))
    def _():
        issue(dnext_ref, 1 - slot)

    def drain(t, c):
        for k in range(2):
            pltpu.make_async_copy(yb_ref.at[pl.ds(0, 1)], buf.at[slot, k, pl.ds(0, 1)], sems.at[slot]).wait()
        return c

    lax.fori_loop(0, tb, drain, 0, unroll=_DMA_UNROLL)

    rec = route_ref[...]
    w1 = rec[:, _R_W1:_R_W1 + 1]
    w2 = rec[:, _R_W2:_R_W2 + 1]
    x2 = x1_ref[...] + (buf[slot, 0] * w1 + buf[slot, 1] * w2)
    ms = jnp.mean(x2 * x2, axis=-1, keepdims=True)
    out_ref[...] = x2 * lax.rsqrt(ms + EPS) * g_ref[...]


def _combine(dest3, yb, x1, route, g, tb):
    T = x1.shape[0]
    n_steps = T // tb
    return pl.pallas_call(
        _combine_kernel,
        grid=(n_steps,),
        in_specs=[
            pl.BlockSpec((1, 1, 2 * tb), lambda i: (i, 0, 0), memory_space=pltpu.SMEM),
            pl.BlockSpec((1, 1, 2 * tb), lambda i: (jnp.minimum(i + 1, n_steps - 1), 0, 0),
                         memory_space=pltpu.SMEM),
            pl.BlockSpec(memory_space=pl.ANY),
            pl.BlockSpec((tb, D_MODEL), lambda i: (i, 0)),
            pl.BlockSpec((tb, LANES), lambda i: (i, 0)),
            pl.BlockSpec((1, D_MODEL), lambda i: (0, 0)),
        ],
        out_specs=pl.BlockSpec((tb, D_MODEL), lambda i: (i, 0)),
        out_shape=jax.ShapeDtypeStruct((T, D_MODEL), F32),
        scratch_shapes=[pltpu.VMEM((2, 2, tb, D_MODEL), F32), pltpu.SemaphoreType.DMA((2,))],
        compiler_params=_cparams(("arbitrary",)),
        name="combine",
    )(dest3, dest3, yb, x1, route, g)


def _tile(T, pref):
    return pref if T % pref == 0 else T


def kernel(x, norm_mix_g, w_in, gmlp_ln_g, gmlp_ln_b, w_spatial, b_spatial, hgrn_lb_logits,
           hgrn_norm_g, w_branch_a, w_branch_b, w_out, norm_ffn_g, w_router_group, b_router_group,
           w_router_expert, b_router_expert, w_expert_gate, w_expert_up, w_expert_down, norm_final_g):
    bn, s, d = x.shape
    T = bn * s
    x2 = x.reshape(T, d)
    l = 0

    lb = jnp.cumsum(jax.nn.softmax(hgrn_lb_logits.astype(F32), axis=0), axis=0)[l][None, :]
    causal = jnp.tril(jnp.ones((GMLP_CHUNK, GMLP_CHUNK), dtype=bool))
    ws_bf = jnp.where(causal[None], w_spatial[l], 0.0).astype(BF16)
    bs_full = jnp.repeat(b_spatial[l].T, GMLP_WIDTH // GMLP_GROUPS, axis=1)
    r = np.arange(SLAB)
    l16 = jnp.asarray(((r[:, None] // HGRN_CHUNK == r[None, :] // HGRN_CHUNK)
                       & (r[None, :] <= r[:, None])).astype(np.float32), dtype=BF16)
    wr = jnp.concatenate([w_router_expert[l], w_router_group[l],
                          jnp.zeros((d, LANES - N_EXPERTS - N_GROUPS), F32)], axis=1).astype(BF16)
    br = jnp.concatenate([b_router_expert[l], b_router_group[l],
                          jnp.zeros((LANES - N_EXPERTS - N_GROUPS,), F32)])[None, :].astype(F32)

    tm1 = _tile(T, 1024)
    zc, f = _inproj(x2, norm_mix_g[l][None, :], w_in[l].astype(BF16), gmlp_ln_g[l][None, :],
                    gmlp_ln_b[l][None, :], ws_bf, bs_full, lb, tm1)

    y_b = _hgrn(zc, f, hgrn_norm_g[l][None, :], l16, _tile(T, 256))

    tm3 = _tile(T, 256)
    rr = np.arange(tm3)
    tri = jnp.asarray((rr[None, :] < rr[:, None]).astype(np.float32), dtype=BF16)
    x1, h2, route, counts = _merge(
        zc, y_b, x2, w_branch_a[l].astype(BF16), w_branch_b[l].astype(BF16), w_out[l].astype(BF16),
        norm_ffn_g[l][None, :], wr, br, tri, tm3)

    counts = counts[0, :N_EXPERTS].astype(jnp.int32)
    padded = ((counts + MOE_BLOCK - 1) // MOE_BLOCK) * MOE_BLOCK
    pad_end = jnp.cumsum(padded)
    pad_start = pad_end - padded
    n_assign = 2 * T
    n_blocks = (n_assign + N_EXPERTS * (MOE_BLOCK - 1) + MOE_BLOCK - 1) // MOE_BLOCK
    block_start = jnp.arange(n_blocks, dtype=jnp.int32) * MOE_BLOCK
    block_e = jnp.minimum(jnp.sum(pad_end[None, :] <= block_start[:, None], axis=1),
                          N_EXPERTS - 1).astype(jnp.int32)
    n_used = (pad_end[-1:] // MOE_BLOCK).astype(jnp.int32)
    first = jnp.concatenate([jnp.ones((1,), jnp.int32),
                             (block_e[1:] != block_e[:-1]).astype(jnp.int32)])
    e_ids = jnp.arange(N_EXPERTS, dtype=jnp.int32)
    later = lax.cummin(jnp.where(padded > 0, e_ids, N_EXPERTS), axis=0, reverse=True)
    next_of = jnp.concatenate([later[1:], jnp.full((1,), N_EXPERTS, jnp.int32)])
    next_of = jnp.where(next_of >= N_EXPERTS, -1, next_of).astype(jnp.int32)
    next_e = next_of[block_e]
    eid = route[:, _R_E1:_R_E2 + 1].astype(jnp.int32)
    rank = route[:, _R_RANK1:_R_RANK2 + 1].astype(jnp.int32)
    dest = jnp.sum(jnp.where(eid[:, :, None] == e_ids, pad_start, 0), axis=-1) + rank
    zstart = pad_start + counts

    tb_d = _tile(T, 1024)
    xb = _dispatch(zstart, pad_end, n_used, dest.reshape(T // tb_d, 1, 2 * tb_d), h2, n_blocks, tb_d)
    yb = _moe(block_e, first, next_e, n_used, xb, w_expert_gate[l], w_expert_up[l], w_expert_down[l])
    tb_c = _tile(T, 512)
    out = _combine(dest.reshape(T // tb_c, 1, 2 * tb_c), yb, x1, route, norm_final_g[None, :], tb_c)
    return out.reshape(bn, s, d)
```

```python
import functools

import jax
import jax.numpy as jnp
import numpy as np
from jax import lax
from jax.experimental import pallas as pl
from jax.experimental.pallas import tpu as pltpu

F32 = jnp.float32
BF16 = jnp.bfloat16

D_MODEL = 2048
EPS = 1e-6
GMLP_WIDTH = 1024
GMLP_GROUPS = 8
GMLP_CHUNK = 128
HGRN_HEADS = 8
HGRN_DIM = 128
HGRN_WIDTH = HGRN_HEADS * HGRN_DIM
HGRN_CHUNK = 16
SLAB = 128
N_GROUPS = 8
EXPERTS_PER_GROUP = 8
N_EXPERTS = N_GROUPS * EXPERTS_PER_GROUP
D_EXPERT = 1024
MOE_BLOCK = 128
COL = 1024
N_COL_STEPS = 10
LANES = 128

VMEM_LIMIT = 56 * 1024 * 1024


def _cparams(sem):
    return pltpu.CompilerParams(dimension_semantics=sem, vmem_limit_bytes=VMEM_LIMIT)


_ROW_CHUNK = 256


def _inproj_kernel(x_ref, g_ref, w_ref, lng_ref, lnb_ref, ws_ref, bs_ref, lb_ref,
                   zc_ref, f_ref, h_scr, u_scr, v_scr):
    j = pl.program_id(1)
    tm = x_ref.shape[0]
    chunks = [slice(r, r + _ROW_CHUNK) for r in range(0, tm, _ROW_CHUNK)]

    def proj(rows):
        return jnp.dot(h_scr[rows, :], w_ref[...], preferred_element_type=F32)

    @pl.when(j == 0)
    def _():
        x = x_ref[...]
        ms = jnp.mean(x * x, axis=-1, keepdims=True)
        h_scr[...] = (x * lax.rsqrt(ms + EPS) * g_ref[...]).astype(BF16)
        for rows in chunks:
            u_scr[rows, :] = jax.nn.gelu(proj(rows))

    @pl.when(j == 1)
    def _():
        for rows in chunks:
            gv = jax.nn.gelu(proj(rows))
            mu = jnp.mean(gv, axis=-1, keepdims=True)
            var = jnp.mean(jnp.square(gv - mu), axis=-1, keepdims=True)
            vn = (gv - mu) * lax.rsqrt(var + EPS) * lng_ref[...] + lnb_ref[...]
            v_scr[rows, :] = vn.astype(BF16)
            for c in range(rows.start, rows.stop, GMLP_CHUNK):
                crows = slice(c, c + GMLP_CHUNK)
                for g in range(GMLP_GROUPS):
                    cols = slice(g * LANES, (g + 1) * LANES)
                    sv = jnp.dot(ws_ref[g], v_scr[crows, cols], preferred_element_type=F32)
                    sv = sv + bs_ref[:, cols]
                    zc_ref[crows, cols] = (u_scr[crows, cols] * sv).astype(BF16)

    @pl.when((j == 2) | (j == 4))
    def _():
        for rows in chunks:
            zc_ref[rows, :] = proj(rows).astype(BF16)

    @pl.when(j == 3)
    def _():
        lb = lb_ref[...]
        for rows in chunks:
            f_ref[rows, :] = lb + (1.0 - lb) * jax.nn.sigmoid(proj(rows))

    @pl.when(j == 5)
    def _():
        for rows in chunks:
            zc_ref[rows, :] = jax.nn.silu(proj(rows)).astype(BF16)

    @pl.when(j >= 6)
    def _():
        for rows in chunks:
            zc_ref[rows, :] = jax.nn.sigmoid(proj(rows)).astype(BF16)


def _zc_col(j):
    return jnp.where(j < 2, 0, jnp.where(j < 4, 1, j - 2))


def _inproj(x2, g, w_bf, lng, lnb, ws_bf, bs_full, lb, tm):
    T = x2.shape[0]
    grid = (T // tm, N_COL_STEPS)
    return pl.pallas_call(
        _inproj_kernel,
        grid=grid,
        in_specs=[
            pl.BlockSpec((tm, D_MODEL), lambda i, j: (i, 0)),
            pl.BlockSpec((1, D_MODEL), lambda i, j: (0, 0)),
            pl.BlockSpec((D_MODEL, COL), lambda i, j: (0, j)),
            pl.BlockSpec((1, COL), lambda i, j: (0, 0)),
            pl.BlockSpec((1, COL), lambda i, j: (0, 0)),
            pl.BlockSpec((GMLP_GROUPS, GMLP_CHUNK, GMLP_CHUNK), lambda i, j: (0, 0, 0)),
            pl.BlockSpec((GMLP_CHUNK, COL), lambda i, j: (0, 0)),
            pl.BlockSpec((1, COL), lambda i, j: (0, 0)),
        ],
        out_specs=[
            pl.BlockSpec((tm, COL), lambda i, j: (i, _zc_col(j))),
            pl.BlockSpec((tm, COL), lambda i, j: (i, 0)),
        ],
        out_shape=[
            jax.ShapeDtypeStruct((T, 8 * COL), BF16),
            jax.ShapeDtypeStruct((T, COL), F32),
        ],
        scratch_shapes=[
            pltpu.VMEM((tm, D_MODEL), BF16),
            pltpu.VMEM((tm, COL), F32),
            pltpu.VMEM((tm, COL), BF16),
        ],
        compiler_params=_cparams(("arbitrary", "arbitrary")),
        name="inproj",
    )(x2, g, w_bf, lng, lnb, ws_bf, bs_full, lb)


_N_CHUNKS = SLAB // HGRN_CHUNK


def _hgrn_kernel(q_ref, f_ref, i_ref, sg_ref, ng_ref, l16_ref, o_ref, st_scr):
    step = pl.program_id(0)

    @pl.when(step == 0)
    def _():
        st_scr[...] = jnp.zeros_like(st_scr)

    rows_per_step = q_ref.shape[0]
    row = lax.broadcasted_iota(jnp.int32, (SLAB, SLAB), 0)
    col = lax.broadcasted_iota(jnp.int32, (SLAB, SLAB), 1)
    rc = row // HGRN_CHUNK
    cc = col // HGRN_CHUNK
    masks = [(rc == cc) & (col <= row)]
    for nb in (1, 2, 4):
        rs = rc // nb
        cs = cc // nb
        masks.append(((rs % 2) == 1) & (cs == rs - 1))

    for sl in range(rows_per_step // SLAB):
        r0 = sl * SLAB
        f = f_ref[r0:r0 + SLAB, :]
        lf = jnp.log(f)
        kk = 1.0 - f
        lf_hi = lf.astype(BF16)
        lf_lo = (lf - lf_hi.astype(F32)).astype(BF16)
        l16 = l16_ref[...]
        e16 = (jnp.dot(l16, lf_hi, preferred_element_type=F32)
               + jnp.dot(l16, lf_lo, preferred_element_type=F32))
        tot = [e16[c * HGRN_CHUNK + HGRN_CHUNK - 1:c * HGRN_CHUNK + HGRN_CHUNK, :]
               for c in range(_N_CHUNKS)]
        r16 = jnp.concatenate(
            [tot[c] - e16[c * HGRN_CHUNK:(c + 1) * HGRN_CHUNK, :] for c in range(_N_CHUNKS)], axis=0)
        qf = q_ref[r0:r0 + SLAB, :].astype(F32)
        q_dec = qf * jnp.exp(e16)
        k_inv = kk * jnp.exp(-e16)
        k_dec = kk * jnp.exp(r16)

        def chunk_sum(lo, hi):
            acc = None
            for c in range(lo, hi):
                acc = tot[c] if acc is None else acc + tot[c]
            return acc

        def scaled(base, offsets):
            parts = []
            for c in range(_N_CHUNKS):
                blk = base[c * HGRN_CHUNK:(c + 1) * HGRN_CHUNK, :]
                if offsets[c] is not None:
                    blk = blk * jnp.exp(offsets[c])
                parts.append(blk)
            return jnp.concatenate(parts, axis=0)

        q_lv = [q_dec, q_dec]
        k_lv = [k_inv, k_dec]
        for nb in (2, 4, 8):
            q_off = [chunk_sum((c // nb) * nb, c) for c in range(_N_CHUNKS)]
            k_off = [chunk_sum(c + 1, (c // nb + 1) * nb) for c in range(_N_CHUNKS)]
            q_lv.append(scaled(q_dec, q_off))
            k_lv.append(scaled(k_dec, k_off))
        slab_decay = jnp.exp(chunk_sum(0, _N_CHUNKS))

        q_bf = [a.astype(BF16) for a in q_lv]
        k_bf = [a.astype(BF16) for a in k_lv]
        i_bf = i_ref[r0:r0 + SLAB, :]

        for h in range(HGRN_HEADS):
            cols = slice(h * HGRN_DIM, (h + 1) * HGRN_DIM)
            scores = None
            for lv in range(4):
                s_lv = lax.dot_general(q_bf[lv][:, cols], k_bf[lv][:, cols],
                                       (((1,), (1,)), ((), ())), preferred_element_type=F32)
                s_lv = jnp.where(masks[lv], s_lv, 0.0)
                scores = s_lv if scores is None else scores + s_lv
            ih = i_bf[:, cols]
            st = st_scr[h]
            o = jnp.dot(scores.astype(BF16), ih, preferred_element_type=F32)
            o = o + lax.dot_general(q_bf[4][:, cols], st.astype(BF16),
                                    (((1,), (1,)), ((), ())), preferred_element_type=F32)
            upd = lax.dot_general(ih, k_bf[4][:, cols],
                                  (((0,), (0,)), ((), ())), preferred_element_type=F32)
            st_scr[h] = st * slab_decay[:, cols] + upd
            ms = jnp.mean(o * o, axis=-1, keepdims=True)
            on = o * lax.rsqrt(ms + EPS) * ng_ref[:, cols]
            o_ref[r0:r0 + SLAB, cols] = (on * sg_ref[r0:r0 + SLAB, cols].astype(F32)).astype(BF16)


def _hgrn(zc, f, ng, l16, rows):
    T = f.shape[0]
    return pl.pallas_call(
        _hgrn_kernel,
        grid=(T // rows,),
        in_specs=[
            pl.BlockSpec((rows, COL), lambda i: (i, 1)),
            pl.BlockSpec((rows, COL), lambda i: (i, 0)),
            pl.BlockSpec((rows, COL), lambda i: (i, 2)),
            pl.BlockSpec((rows, COL), lambda i: (i, 3)),
            pl.BlockSpec((1, COL), lambda i: (0, 0)),
            pl.BlockSpec((SLAB, SLAB), lambda i: (0, 0)),
        ],
        out_specs=pl.BlockSpec((rows, COL), lambda i: (i, 0)),
        out_shape=jax.ShapeDtypeStruct((T, COL), BF16),
        scratch_shapes=[pltpu.VMEM((HGRN_HEADS, HGRN_DIM, HGRN_DIM), F32)],
        compiler_params=_cparams(("arbitrary",)),
        name="hgrn",
    )(zc, f, zc, zc, ng, l16)


_R_E1, _R_E2, _R_W1, _R_W2, _R_RANK1, _R_RANK2 = range(6)
_GROUP_LANE0 = N_EXPERTS


def _merge_kernel(ya_ref, yb_ref, ga0_ref, ga1_ref, gb0_ref, gb1_ref, x_ref,
                  wa_ref, wb_ref, wo_ref, g_ref, wr_ref, br_ref, tri_ref,
                  x1_ref, h2_ref, route_ref, cnt_ref, cnt_scr):
    step = pl.program_id(0)

    @pl.when(step == 0)
    def _():
        cnt_scr[...] = jnp.zeros_like(cnt_scr)

    tm = x_ref.shape[0]
    parts = []
    for r in range(0, tm, tm // 2):
        rows = slice(r, r + tm // 2)
        pa = jnp.dot(ya_ref[rows, :], wa_ref[...], preferred_element_type=F32)
        pb = jnp.dot(yb_ref[rows, :], wb_ref[...], preferred_element_type=F32)
        ga = jnp.concatenate([ga0_ref[rows, :], ga1_ref[rows, :]], axis=1).astype(F32)
        gb = jnp.concatenate([gb0_ref[rows, :], gb1_ref[rows, :]], axis=1).astype(F32)
        merged = ga * pa + gb * pb
        x1 = x_ref[rows, :] + jnp.dot(merged.astype(BF16), wo_ref[...], preferred_element_type=F32)
        x1_ref[rows, :] = x1
        ms = jnp.mean(x1 * x1, axis=-1, keepdims=True)
        h2 = x1 * lax.rsqrt(ms + EPS) * g_ref[...]
        h2_ref[rows, :] = h2
        parts.append(jnp.dot(h2.astype(BF16), wr_ref[...], preferred_element_type=F32) + br_ref[...])
    logits = jnp.concatenate(parts, axis=0)

    lane = lax.broadcasted_iota(jnp.int32, (tm, LANES), 1).astype(F32)
    neg = jnp.float32(-jnp.inf)
    big = jnp.float32(1e9)
    is_group = (lane >= _GROUP_LANE0) & (lane < _GROUP_LANE0 + N_GROUPS)
    gl = jnp.where(is_group, logits, neg)
    gmax = jnp.max(gl, axis=-1, keepdims=True)
    g_sel = jnp.min(jnp.where(gl == gmax, lane, big), axis=-1, keepdims=True) - _GROUP_LANE0
    p_g = 1.0 / jnp.sum(jnp.exp(gl - gmax), axis=-1, keepdims=True)
    lo = g_sel * EXPERTS_PER_GROUP
    in_group = (lane >= lo) & (lane < lo + EXPERTS_PER_GROUP)
    el = jnp.where(in_group, logits, neg)
    m1 = jnp.max(el, axis=-1, keepdims=True)
    e1 = jnp.min(jnp.where(el == m1, lane, big), axis=-1, keepdims=True)
    el2 = jnp.where(lane == e1, neg, el)
    m2 = jnp.max(el2, axis=-1, keepdims=True)
    e2 = jnp.min(jnp.where(el2 == m2, lane, big), axis=-1, keepdims=True)
    ex = jnp.exp(m2 - m1)
    w1 = p_g / (1.0 + ex)
    w2 = p_g * ex / (1.0 + ex)

    oh1 = (lane == e1).astype(F32)
    oh2 = (lane == e2).astype(F32)
    both = oh1 + oh2
    before = jnp.dot(tri_ref[...], both.astype(BF16), preferred_element_type=F32) + cnt_scr[...]
    rank1 = jnp.sum(oh1 * before, axis=-1, keepdims=True)
    rank2 = jnp.sum(oh2 * before, axis=-1, keepdims=True)
    cnt = cnt_scr[...] + jnp.sum(both, axis=0, keepdims=True)
    cnt_scr[...] = cnt
    cnt_ref[...] = cnt

    rec = jnp.zeros((tm, LANES), F32)
    for idx, val in ((_R_E1, e1), (_R_E2, e2), (_R_W1, w1), (_R_W2, w2),
                     (_R_RANK1, rank1), (_R_RANK2, rank2)):
        rec = jnp.where(lane == idx, val, rec)
    route_ref[...] = rec


def _merge(zc, yb, x2, wa, wb, wo, g, wr, br, tri, tm):
    T = x2.shape[0]
    const = lambda shape: pl.BlockSpec(shape, lambda i: (0,) * len(shape),
                                       pipeline_mode=pl.Buffered(1))
    return pl.pallas_call(
        _merge_kernel,
        grid=(T // tm,),
        in_specs=[
            pl.BlockSpec((tm, COL), lambda i: (i, 0)),
            pl.BlockSpec((tm, COL), lambda i: (i, 0)),
            pl.BlockSpec((tm, COL), lambda i: (i, 4)),
            pl.BlockSpec((tm, COL), lambda i: (i, 5)),
            pl.BlockSpec((tm, COL), lambda i: (i, 6)),
            pl.BlockSpec((tm, COL), lambda i: (i, 7)),
            pl.BlockSpec((tm, D_MODEL), lambda i: (i, 0)),
            const((GMLP_WIDTH, D_MODEL)),
            const((HGRN_WIDTH, D_MODEL)),
            const((D_MODEL, D_MODEL)),
            const((1, D_MODEL)),
            const((D_MODEL, LANES)),
            const((1, LANES)),
            const((tm, tm)),
        ],
        out_specs=[
            pl.BlockSpec((tm, D_MODEL), lambda i: (i, 0)),
            pl.BlockSpec((tm, D_MODEL), lambda i: (i, 0)),
            pl.BlockSpec((tm, LANES), lambda i: (i, 0)),
            pl.BlockSpec((1, LANES), lambda i: (0, 0)),
        ],
        out_shape=[
            jax.ShapeDtypeStruct((T, D_MODEL), F32),
            jax.ShapeDtypeStruct((T, D_MODEL), F32),
            jax.ShapeDtypeStruct((T, LANES), F32),
            jax.ShapeDtypeStruct((1, LANES), F32),
        ],
        scratch_shapes=[pltpu.VMEM((1, LANES), F32)],
        compiler_params=_cparams(("arbitrary",)),
        name="merge_route",
    )(zc, yb, zc, zc, zc, zc, x2, wa, wb, wo, g, wr, br, tri)


_DMA_UNROLL = 8


def _dispatch_kernel(zstart_ref, zend_ref, nused_ref, dest_ref, h_ref, xb_ref, zeros_scr, sem, zsem):
    tb = h_ref.shape[0]
    total_blocks = xb_ref.shape[0] // MOE_BLOCK

    @pl.when(pl.program_id(0) == 0)
    def _():
        zeros_scr[...] = jnp.zeros_like(zeros_scr)

        def for_each(fn):
            def tail(j, c):
                row = pl.multiple_of(j * MOE_BLOCK, MOE_BLOCK)
                fn(pltpu.make_async_copy(zeros_scr, xb_ref.at[pl.ds(row, MOE_BLOCK)], zsem))
                return c

            def expert(e, c):
                zs = zstart_ref[e]
                up = lax.shift_left(lax.shift_right_logical(zs + 7, 3), 3)

                def one(r, c2):
                    fn(pltpu.make_async_copy(zeros_scr.at[pl.ds(0, 1)], xb_ref.at[pl.ds(r, 1)], zsem))
                    return c2
                lax.fori_loop(zs, up, one, 0)
                rem = zend_ref[e] - up
                off = up
                for size in (64, 32, 16, 8):
                    has = (rem & size) != 0

                    @pl.when(has)
                    def _(off=off, size=size):
                        fn(pltpu.make_async_copy(zeros_scr.at[pl.ds(0, size)],
                                                 xb_ref.at[pl.ds(pl.multiple_of(off, 8), size)], zsem))
                    off = off + jnp.where(has, size, 0)
                return c

            lax.fori_loop(nused_ref[0], total_blocks, tail, 0)
            lax.fori_loop(0, N_EXPERTS, expert, 0)

        for_each(lambda cp: cp.start())
        for_each(lambda cp: cp.wait())

    def start(t, c):
        for k in range(2):
            d = dest_ref[0, 0, 2 * t + k]
            pltpu.make_async_copy(h_ref.at[pl.ds(t, 1)], xb_ref.at[pl.ds(d, 1)], sem).start(priority=k)
        return c

    def wait(t, c):
        for k in range(2):
            pltpu.make_async_copy(h_ref.at[pl.ds(0, 1)], xb_ref.at[pl.ds(0, 1)], sem).wait()
        return c

    lax.fori_loop(0, tb, start, 0, unroll=_DMA_UNROLL)
    lax.fori_loop(0, tb, wait, 0, unroll=_DMA_UNROLL)


def _dispatch(zstart, zend, n_used, dest3, h2, n_blocks, tb):
    T = h2.shape[0]
    grid_spec = pltpu.PrefetchScalarGridSpec(
        num_scalar_prefetch=3,
        grid=(T // tb,),
        in_specs=[
            pl.BlockSpec((1, 1, 2 * tb), lambda i, zs, ze, nu: (i, 0, 0), memory_space=pltpu.SMEM),
            pl.BlockSpec((tb, D_MODEL), lambda i, zs, ze, nu: (i, 0)),
        ],
        out_specs=pl.BlockSpec(memory_space=pl.ANY),
        scratch_shapes=[pltpu.VMEM((MOE_BLOCK, D_MODEL), F32),
                        pltpu.SemaphoreType.DMA(()), pltpu.SemaphoreType.DMA(())],
    )
    return pl.pallas_call(
        _dispatch_kernel,
        grid_spec=grid_spec,
        out_shape=jax.ShapeDtypeStruct((n_blocks * MOE_BLOCK, D_MODEL), F32),
        compiler_params=_cparams(("arbitrary",)),
        name="dispatch",
    )(zstart, zend, n_used, dest3, h2)


def _moe_kernel(be_ref, first_ref, nexte_ref, nused_ref, x_ref, wg_hbm, wu_hbm, wd_hbm, y_ref,
                stg_g, stg_u, stg_d, wg_bf, wu_bf, wd_bf, sems):
    b = pl.program_id(0)
    used = b < nused_ref[0]

    streams = ((wg_hbm, stg_g, wg_bf), (wu_hbm, stg_u, wu_bf), (wd_hbm, stg_d, wd_bf))

    def weight_copy(i, e):
        hbm, stg, _ = streams[i]
        return pltpu.make_async_copy(hbm.at[e], stg, sems.at[i])

    @pl.when(b == 0)
    def _():
        for i in range(3):
            weight_copy(i, be_ref[0]).start(priority=i % 2)

    @pl.when(used & (first_ref[b] == 1))
    def _():
        nxt = nexte_ref[b]
        for i in range(3):
            weight_copy(i, be_ref[b]).wait()
            streams[i][2][...] = streams[i][1][...].astype(BF16)

            @pl.when(nxt >= 0)
            def _():
                weight_copy(i, nxt).start(priority=i % 2)

    @pl.when(used)
    def _():
        x = x_ref[...].astype(BF16)
        a = jnp.dot(x, wg_bf[...], preferred_element_type=F32)
        g = jnp.dot(x, wu_bf[...], preferred_element_type=F32)
        hdn = (jax.nn.silu(a) * g).astype(BF16)
        y_ref[...] = jnp.dot(hdn, wd_bf[...], preferred_element_type=F32)

    @pl.when(jnp.logical_not(used))
    def _():
        y_ref[...] = jnp.zeros_like(y_ref)


def _moe(block_e, first, next_e, n_used, xb, wg, wu, wd):
    n_blocks = block_e.shape[0]
    P = n_blocks * MOE_BLOCK
    grid_spec = pltpu.PrefetchScalarGridSpec(
        num_scalar_prefetch=4,
        grid=(n_blocks,),
        in_specs=[
            pl.BlockSpec((MOE_BLOCK, D_MODEL), lambda b, be, fi, ne, nu: (jnp.minimum(b, nu[0] - 1), 0)),
            pl.BlockSpec(memory_space=pl.ANY),
            pl.BlockSpec(memory_space=pl.ANY),
            pl.BlockSpec(memory_space=pl.ANY),
        ],
        out_specs=pl.BlockSpec((MOE_BLOCK, D_MODEL), lambda b, be, fi, ne, nu: (b, 0)),
        scratch_shapes=[
            pltpu.VMEM((D_MODEL, D_EXPERT), F32),
            pltpu.VMEM((D_MODEL, D_EXPERT), F32),
            pltpu.VMEM((D_EXPERT, D_MODEL), F32),
            pltpu.VMEM((D_MODEL, D_EXPERT), BF16),
            pltpu.VMEM((D_MODEL, D_EXPERT), BF16),
            pltpu.VMEM((D_EXPERT, D_MODEL), BF16),
            pltpu.SemaphoreType.DMA((3,)),
        ],
    )
    return pl.pallas_call(
        _moe_kernel,
        grid_spec=grid_spec,
        out_shape=jax.ShapeDtypeStruct((P, D_MODEL), F32),
        compiler_params=_cparams(("arbitrary",)),
        name="moe_experts",
    )(block_e, first, next_e, n_used, xb, wg, wu, wd)


def _combine_kernel(dest_ref, dnext_ref, yb_ref, x1_ref, route_ref, g_ref, out_ref, buf, sems):
    tb = x1_ref.shape[0]
    step = pl.program_id(0)
    slot = step % 2

    def issue(d_ref, s):
        def body(t, c):
            for k in range(2):
                d = d_ref[0, 0, 2 * t + k]
                pltpu.make_async_copy(yb_ref.at[pl.ds(d, 1)], buf.at[s, k, pl.ds(t, 1)],
                                      sems.at[s]).start(priority=k)
            return c
        lax.fori_loop(0, tb, body, 0, unroll=_DMA_UNROLL)

    @pl.when(step == 0)
    def _():
        issue(dest_ref, 0)

    @pl.when(step + 1 < pl.num_programs(0))
    def _():
        issue(dnext_ref, 1 - slot)

    def drain(t, c):
        for k in range(2):
            pltpu.make_async_copy(yb_ref.at[pl.ds(0, 1)], buf.at[slot, k, pl.ds(0, 1)], sems.at[slot]).wait()
        return c

    lax.fori_loop(0, tb, drain, 0, unroll=_DMA_UNROLL)

    rec = route_ref[...]
    w1 = rec[:, _R_W1:_R_W1 + 1]
    w2 = rec[:, _R_W2:_R_W2 + 1]
    x2 = x1_ref[...] + (buf[slot, 0] * w1 + buf[slot, 1] * w2)
    ms = jnp.mean(x2 * x2, axis=-1, keepdims=True)
    out_ref[...] = x2 * lax.rsqrt(ms + EPS) * g_ref[...]


def _combine(dest3, yb, x1, route, g, tb):
    T = x1.shape[0]
    n_steps = T // tb
    return pl.pallas_call(
        _combine_kernel,
        grid=(n_steps,),
        in_specs=[
            pl.BlockSpec((1, 1, 2 * tb), lambda i: (i, 0, 0), memory_space=pltpu.SMEM),
            pl.BlockSpec((1, 1, 2 * tb), lambda i: (jnp.minimum(i + 1, n_steps - 1), 0, 0),
                         memory_space=pltpu.SMEM),
            pl.BlockSpec(memory_space=pl.ANY),
            pl.BlockSpec((tb, D_MODEL), lambda i: (i, 0)),
            pl.BlockSpec((tb, LANES), lambda i: (i, 0)),
            pl.BlockSpec((1, D_MODEL), lambda i: (0, 0)),
        ],
        out_specs=pl.BlockSpec((tb, D_MODEL), lambda i: (i, 0)),
        out_shape=jax.ShapeDtypeStruct((T, D_MODEL), F32),
        scratch_shapes=[pltpu.VMEM((2, 2, tb, D_MODEL), F32), pltpu.SemaphoreType.DMA((2,))],
        compiler_params=_cparams(("arbitrary",)),
        name="combine",
    )(dest3, dest3, yb, x1, route, g)


def _tile(T, pref):
    return pref if T % pref == 0 else T


def kernel(x, norm_mix_g, w_in, gmlp_ln_g, gmlp_ln_b, w_spatial, b_spatial, hgrn_lb_logits,
           hgrn_norm_g, w_branch_a, w_branch_b, w_out, norm_ffn_g, w_router_group, b_router_group,
           w_router_expert, b_router_expert, w_expert_gate, w_expert_up, w_expert_down, norm_final_g):
    bn, s, d = x.shape
    T = bn * s
    x2 = x.reshape(T, d)
    l = 0

    lb = jnp.cumsum(jax.nn.softmax(hgrn_lb_logits.astype(F32), axis=0), axis=0)[l][None, :]
    causal = jnp.tril(jnp.ones((GMLP_CHUNK, GMLP_CHUNK), dtype=bool))
    ws_bf = jnp.where(causal[None], w_spatial[l], 0.0).astype(BF16)
    bs_full = jnp.repeat(b_spatial[l].T, GMLP_WIDTH // GMLP_GROUPS, axis=1)
    r = np.arange(SLAB)
    l16 = jnp.asarray(((r[:, None] // HGRN_CHUNK == r[None, :] // HGRN_CHUNK)
                       & (r[None, :] <= r[:, None])).astype(np.float32), dtype=BF16)
    wr = jnp.concatenate([w_router_expert[l], w_router_group[l],
                          jnp.zeros((d, LANES - N_EXPERTS - N_GROUPS), F32)], axis=1).astype(BF16)
    br = jnp.concatenate([b_router_expert[l], b_router_group[l],
                          jnp.zeros((LANES - N_EXPERTS - N_GROUPS,), F32)])[None, :].astype(F32)

    tm1 = _tile(T, 1024)
    zc, f = _inproj(x2, norm_mix_g[l][None, :], w_in[l].astype(BF16), gmlp_ln_g[l][None, :],
                    gmlp_ln_b[l][None, :], ws_bf, bs_full, lb, tm1)

    y_b = _hgrn(zc, f, hgrn_norm_g[l][None, :], l16, _tile(T, 256))

    tm3 = _tile(T, 256)
    rr = np.arange(tm3)
    tri = jnp.asarray((rr[None, :] < rr[:, None]).astype(np.float32), dtype=BF16)
    x1, h2, route, counts = _merge(
        zc, y_b, x2, w_branch_a[l].astype(BF16), w_branch_b[l].astype(BF16), w_out[l].astype(BF16),
        norm_ffn_g[l][None, :], wr, br, tri, tm3)

    counts = counts[0, :N_EXPERTS].astype(jnp.int32)
    padded = ((counts + MOE_BLOCK - 1) // MOE_BLOCK) * MOE_BLOCK
    pad_end = jnp.cumsum(padded)
    pad_start = pad_end - padded
    n_assign = 2 * T
    n_blocks = (n_assign + N_EXPERTS * (MOE_BLOCK - 1) + MOE_BLOCK - 1) // MOE_BLOCK
    block_start = jnp.arange(n_blocks, dtype=jnp.int32) * MOE_BLOCK
    block_e = jnp.minimum(jnp.sum(pad_end[None, :] <= block_start[:, None], axis=1),
                          N_EXPERTS - 1).astype(jnp.int32)
    n_used = (pad_end[-1:] // MOE_BLOCK).astype(jnp.int32)
    first = jnp.concatenate([jnp.ones((1,), jnp.int32),
                             (block_e[1:] != block_e[:-1]).astype(jnp.int32)])
    e_ids = jnp.arange(N_EXPERTS, dtype=jnp.int32)
    later = lax.cummin(jnp.where(padded > 0, e_ids, N_EXPERTS), axis=0, reverse=True)
    next_of = jnp.concatenate([later[1:], jnp.full((1,), N_EXPERTS, jnp.int32)])
    next_of = jnp.where(next_of >= N_EXPERTS, -1, next_of).astype(jnp.int32)
    next_e = next_of[block_e]
    eid = route[:, _R_E1:_R_E2 + 1].astype(jnp.int32)
    rank = route[:, _R_RANK1:_R_RANK2 + 1].astype(jnp.int32)
    dest = jnp.sum(jnp.where(eid[:, :, None] == e_ids, pad_start, 0), axis=-1) + rank
    zstart = pad_start + counts

    tb_d = _tile(T, 1024)
    xb = _dispatch(zstart, pad_end, n_used, dest.reshape(T // tb_d, 1, 2 * tb_d), h2, n_blocks, tb_d)
    yb = _moe(block_e, first, next_e, n_used, xb, w_expert_gate[l], w_expert_up[l], w_expert_down[l])
    tb_c = _tile(T, 512)
    out = _combine(dest.reshape(T // tb_c, 1, 2 * tb_c), yb, x1, route, norm_final_g[None, :], tb_c)
    return out.reshape(bn, s, d)
```
